```python
import math
import jax, jax.numpy as jnp
from jax import lax
import numpy as np

D_MODEL = 4096
BATCH = 4
SEQ = 2048
DEPTH = 2
DEC_BATCH = 128
DEC_SEQ = 8
PAST_LEN = 16384
PAGE_SIZE = 128

N_BRANCH = 3
D_MIX = D_MODEL // 2
GDN_DK = 128
GDN_HEADS = D_MIX // 128
GDN_DV = D_MIX // GDN_HEADS
GDN_QK = GDN_HEADS * GDN_DK
CONV_K = 4
CONV_DIM = 2 * GDN_QK + D_MIX
RET_DV = 256
RET_HEADS = D_MIX // RET_DV
RET_DK = RET_DV // 2
RET_QK = RET_HEADS * RET_DK
ROPE_BASE = 10000.0
GLA_HEADS = 4
GLA_DV = D_MIX // GLA_HEADS
GLA_DK = GLA_DV // 2
GLA_QK = GLA_HEADS * GLA_DK
GLA_RANK = 16
GLA_TAU = 16.0
CHUNK = 64
GLA_CHUNK = 16
D_FF = 256 * ((8 * D_MODEL // 3 + 255) // 256)
N_IN = (CONV_DIM + D_MIX + 2 * GDN_HEADS + 2 * RET_QK + 2 * D_MIX
        + 2 * GLA_QK + 2 * D_MIX + GLA_RANK + N_BRANCH * D_MODEL)
EPS = 1e-6

kernel_name = "hybrid_gdn_retnet_gla_macaron_step"


def rmsnorm(x, w):
    xf = x.astype(jnp.float32)
    y = xf * lax.rsqrt(jnp.mean(xf * xf, -1, keepdims=True) + EPS)
    return (y * w.astype(jnp.float32)).astype(x.dtype)


def _rms(x):
    return x * lax.rsqrt(jnp.mean(x * x, -1, keepdims=True) + EPS)


def _l2norm(x):
    return x * lax.rsqrt(jnp.sum(x * x, -1, keepdims=True) + EPS)


def swiglu(x, wg, wu, wd):
    return (jax.nn.silu(x @ wg) * (x @ wu)) @ wd


def _split_heads(x, h):
    b, l, _ = x.shape
    return x.reshape(b, l, h, -1).transpose(0, 2, 1, 3)


def _merge_heads(x):
    b, h, l, d = x.shape
    return x.transpose(0, 2, 1, 3).reshape(b, l, h * d)


def _chunks(x, c):
    b, h, l, d = x.shape
    return x.reshape(b, h, l // c, c, d)


def _lead(x):
    return jnp.moveaxis(x, 2, 0)


def _unchunk(o):
    n, b, h, c, d = o.shape
    return o.transpose(1, 2, 0, 3, 4).reshape(b, h, n * c, d)


def rope(x, pos):
    half = x.shape[-1] // 2
    inv = ROPE_BASE ** (-jnp.arange(half, dtype=jnp.float32) / half)
    ang = pos.astype(jnp.float32)[:, None] * inv[None, :]
    cos, sin = jnp.cos(ang), jnp.sin(ang)
    x1, x2 = x[..., :half], x[..., half:]
    return jnp.concatenate([x1 * cos - x2 * sin, x1 * sin + x2 * cos], axis=-1)


def causal_conv(x, buf, w):
    l = x.shape[1]
    xx = jnp.concatenate([buf.astype(x.dtype), x], axis=1)
    y = xx[:, 0:l] * w[0]
    for i in range(1, CONV_K):
        y = y + xx[:, i:i + l] * w[i]
    return jax.nn.silu(y), xx[:, xx.shape[1] - (CONV_K - 1):]


def gated_delta_rule(q, k, v, g, beta, s0):
    b, h, l, dk = q.shape
    c = math.gcd(l, CHUNK)
    n = l // c
    qc, kc, vc = _chunks(q, c), _chunks(k, c), _chunks(v, c)
    gc = jnp.cumsum(g.reshape(b, h, n, c), axis=-1)
    bc = beta.reshape(b, h, n, c)[..., None]
    incl = jnp.tril(jnp.ones((c, c), bool))
    strict = jnp.tril(jnp.ones((c, c), bool), -1)
    decay = jnp.exp(jnp.where(incl, gc[..., :, None] - gc[..., None, :], -jnp.inf))
    kb = kc * bc
    a = jnp.where(strict, jnp.einsum("bhnid,bhnjd->bhnij", kb, kc) * decay, 0.0)
    eye = jnp.eye(c, dtype=q.dtype)
    t = lax.linalg.triangular_solve(a + eye, jnp.broadcast_to(eye, a.shape),
                                    left_side=True, lower=True, unit_diagonal=True)
    u = jnp.einsum("bhnij,bhnjd->bhnid", t, vc * bc)
    w = jnp.einsum("bhnij,bhnjd->bhnid", t, kb * jnp.exp(gc)[..., None])
    qk = jnp.einsum("bhnid,bhnjd->bhnij", qc, kc) * decay
    g_last = gc[..., -1]

    def step(s, xs):
        q_i, k_i, u_i, w_i, qk_i, gc_i, gl_i = xs
        v_new = u_i - jnp.einsum("bhcd,bhde->bhce", w_i, s)
        o_i = (jnp.einsum("bhcd,bhde->bhce", q_i * jnp.exp(gc_i)[..., None], s)
               + jnp.einsum("bhij,bhje->bhie", qk_i, v_new))
        s = (s * jnp.exp(gl_i)[..., None, None]
             + jnp.einsum("bhcd,bhce->bhde", k_i * jnp.exp(gl_i[..., None] - gc_i)[..., None], v_new))
        return s, o_i

    xs = (_lead(qc), _lead(kc), _lead(u), _lead(w), _lead(qk), _lead(gc), _lead(g_last))
    s, o = lax.scan(step, s0, xs)
    return _unchunk(o), s


def retention(q, k, v, log_gamma, s0):
    b, h, l, dk = q.shape
    c = math.gcd(l, CHUNK)
    qc, kc, vc = _chunks(q, c), _chunks(k, c), _chunks(v, c)
    pos = jnp.arange(c, dtype=jnp.float32)
    incl = jnp.tril(jnp.ones((c, c), bool))
    lg = log_gamma[:, None, None]
    decay = jnp.exp(jnp.where(incl, (pos[:, None] - pos[None, :]) * lg, -jnp.inf))
    scores = jnp.einsum("bhnid,bhnjd->bhnij", qc, kc) * decay[:, None]
    o_intra = jnp.einsum("bhnij,bhnje->bhnie", scores, vc)
    q_dec = qc * jnp.exp((pos + 1.0)[None, :] * log_gamma[:, None])[None, :, None, :, None]
    k_dec = kc * jnp.exp((c - 1.0 - pos)[None, :] * log_gamma[:, None])[None, :, None, :, None]
    chunk_decay = jnp.exp(c * log_gamma)[None, :, None, None]

    def step(s, xs):
        qd_i, kd_i, v_i, oi_i = xs
        o_i = jnp.einsum("bhcd,bhde->bhce", qd_i, s) + oi_i
        s = s * chunk_decay + jnp.einsum("bhcd,bhce->bhde", kd_i, v_i)
        return s, o_i

    s, o = lax.scan(step, s0, (_lead(q_dec), _lead(k_dec), _lead(vc), _lead(o_intra)))
    return _unchunk(o), s


def gla(q, k, v, log_a, s0):
    b, h, l, dk = q.shape
    c = math.gcd(l, GLA_CHUNK)
    qc, kc, vc, ac = _chunks(q, c), _chunks(k, c), _chunks(v, c), _chunks(log_a, c)
    bcum = jnp.cumsum(ac, axis=-2)
    qd = qc * jnp.exp(bcum)
    kd = kc * jnp.exp(-bcum)
    incl = jnp.tril(jnp.ones((c, c), bool))
    attn = jnp.where(incl, jnp.einsum("bhnid,bhnjd->bhnij", qd, kd), 0.0)
    o_intra = jnp.einsum("bhnij,bhnje->bhnie", attn, vc)
    b_last = bcum[..., -1:, :]
    kl = kc * jnp.exp(b_last - bcum)
    al = jnp.exp(b_last[..., 0, :])

    def step(s, xs):
        qd_i, kl_i, v_i, oi_i, al_i = xs
        o_i = jnp.einsum("bhcd,bhde->bhce", qd_i, s) + oi_i
        s = s * al_i[..., None] + jnp.einsum("bhcd,bhce->bhde", kl_i, v_i)
        return s, o_i

    s, o = lax.scan(step, s0, (_lead(qd), _lead(kl), _lead(vc), _lead(o_intra), _lead(al)))
    return _unchunk(o), s


def decoder_layer(x, pos, s_gdn, s_conv, s_ret, s_gla,
                  norm_ffn1, ffn1_gate, ffn1_up, ffn1_down, norm_mix, w_in,
                  gdn_conv_w, gdn_a_log, gdn_dt_bias, gdn_norm_w, gla_w_a2, gla_b_a,
                  gla_norm_w, w_branch, w_out, norm_ffn2, ffn2_gate, ffn2_up, ffn2_down):
    f32 = jnp.float32
    dt = x.dtype
    bsz, l, _ = x.shape
    x = x + 0.5 * swiglu(rmsnorm(x, norm_ffn1), ffn1_gate, ffn1_up, ffn1_down)
    h = rmsnorm(x, norm_mix)
    proj = h @ w_in
    sizes = (CONV_DIM, D_MIX, GDN_HEADS, GDN_HEADS, RET_QK, RET_QK, D_MIX, D_MIX,
             GLA_QK, GLA_QK, D_MIX, D_MIX, GLA_RANK, N_BRANCH * D_MODEL)
    (qkv_a, z_a, a_a, b_a, q_b, k_b, v_b, g_b, q_c, k_c, v_c, r_c, lr_c,
     gate_logit) = jnp.split(proj, np.cumsum(sizes)[:-1].tolist(), axis=-1)

    qkv_a, new_conv = causal_conv(qkv_a, s_conv, gdn_conv_w)
    q_a, k_a, v_a = jnp.split(qkv_a.astype(f32), [GDN_QK, 2 * GDN_QK], axis=-1)
    q_a = _l2norm(_split_heads(q_a, GDN_HEADS)) * GDN_DK ** -0.5
    k_a = _l2norm(_split_heads(k_a, GDN_HEADS))
    v_a = _split_heads(v_a, GDN_HEADS)
    beta = jax.nn.sigmoid(b_a.astype(f32)).transpose(0, 2, 1)
    g_a = (-jnp.exp(gdn_a_log.astype(f32))
           * jax.nn.softplus(a_a.astype(f32) + gdn_dt_bias.astype(f32))).transpose(0, 2, 1)
    o_a, new_gdn = gated_delta_rule(q_a, k_a, v_a, g_a, beta, s_gdn.astype(f32))
    o_a = _merge_heads(_rms(o_a) * gdn_norm_w.astype(f32)
                       * jax.nn.silu(_split_heads(z_a.astype(f32), GDN_HEADS)))

    q_b = rope(_split_heads(q_b.astype(f32), RET_HEADS), pos)
    k_b = rope(_split_heads(k_b.astype(f32), RET_HEADS), pos) * RET_DK ** -0.5
    v_b = _split_heads(v_b.astype(f32), RET_HEADS)
    log_gamma = jnp.log1p(-jnp.exp2(-5.0 - jnp.arange(RET_HEADS, dtype=f32)))
    o_b, new_ret = retention(q_b, k_b, v_b, log_gamma, s_ret.astype(f32))
    mu = jnp.mean(o_b, -1, keepdims=True)
    var = jnp.mean(jnp.square(o_b - mu), -1, keepdims=True)
    o_b = _merge_heads((o_b - mu) * lax.rsqrt(var + 1e-5)) * jax.nn.silu(g_b.astype(f32))

    q_c = _split_heads(q_c.astype(f32), GLA_HEADS) * GLA_DK ** -0.5
    k_c = _split_heads(k_c.astype(f32), GLA_HEADS)
    v_c = _split_heads(v_c.astype(f32), GLA_HEADS)
    log_a = jax.nn.log_sigmoid((lr_c @ gla_w_a2 + gla_b_a).astype(f32)) / GLA_TAU
    o_c, new_gla = gla(q_c, k_c, v_c, _split_heads(log_a, GLA_HEADS), s_gla.astype(f32))
    o_c = _merge_heads(_rms(o_c) * gla_norm_w.astype(f32)) * jax.nn.silu(r_c.astype(f32))

    branches = jnp.stack([o_a, o_b, o_c], axis=2).astype(dt)
    gates = jax.nn.sigmoid(gate_logit.reshape(bsz, l, N_BRANCH, D_MODEL))
    merged = jnp.einsum("blrd,blrd->bld", gates,
                        jnp.einsum("blrm,rmd->blrd", branches, w_branch))
    x = x + merged @ w_out
    x = x + 0.5 * swiglu(rmsnorm(x, norm_ffn2), ffn2_gate, ffn2_up, ffn2_down)
    return x, (new_gdn.astype(s_gdn.dtype), new_conv.astype(s_conv.dtype),
               new_ret.astype(s_ret.dtype), new_gla.astype(s_gla.dtype))


def setup_inputs(seed: int = 0) -> dict:
    key = jax.random.key(seed)
    ks = iter(jax.random.split(key, 40))
    f32 = jnp.float32

    def nrm(shape, scale):
        return jax.random.normal(next(ks), shape, f32) * scale

    def gain(shape):
        return 1.0 + nrm(shape, 0.01)

    dt_init = jnp.exp(jax.random.uniform(next(ks), (DEPTH, GDN_HEADS), f32,
                                         math.log(1e-3), math.log(1e-1)))
    return {
        "x_prompt": nrm((BATCH, SEQ, D_MODEL), 1.0),
        "x_sample": nrm((DEC_BATCH, DEC_SEQ, D_MODEL), 1.0),
        "state_gdn": nrm((DEPTH, DEC_BATCH, GDN_HEADS, GDN_DK, GDN_DV), 0.1),
        "state_gdn_conv": nrm((DEPTH, DEC_BATCH, CONV_K - 1, CONV_DIM), 1.0),
        "state_ret": nrm((DEPTH, DEC_BATCH, RET_HEADS, RET_DK, RET_DV), 0.5),
        "state_gla": nrm((DEPTH, DEC_BATCH, GLA_HEADS, GLA_DK, GLA_DV), 0.5),
        "norm_ffn1": gain((DEPTH, D_MODEL)),
        "ffn1_gate": nrm((DEPTH, D_MODEL, D_FF), D_MODEL ** -0.5),
        "ffn1_up": nrm((DEPTH, D_MODEL, D_FF), D_MODEL ** -0.5),
        "ffn1_down": nrm((DEPTH, D_FF, D_MODEL), D_FF ** -0.5),
        "norm_mix": gain((DEPTH, D_MODEL)),
        "w_in": nrm((DEPTH, D_MODEL, N_IN), D_MODEL ** -0.5),
        "gdn_conv_w": nrm((DEPTH, CONV_K, CONV_DIM), CONV_K ** -0.5),
        "gdn_a_log": jnp.log(jax.random.uniform(next(ks), (DEPTH, GDN_HEADS), f32, 1.0, 16.0)),
        "gdn_dt_bias": dt_init + jnp.log(-jnp.expm1(-dt_init)),
        "gdn_norm_w": gain((DEPTH, GDN_DV)),
        "gla_w_a2": nrm((DEPTH, GLA_RANK, GLA_QK), GLA_RANK ** -0.5),
        "gla_b_a": nrm((DEPTH, GLA_QK), 0.01),
        "gla_norm_w": gain((DEPTH, GLA_DV)),
        "w_branch": nrm((DEPTH, N_BRANCH, D_MIX, D_MODEL), D_MIX ** -0.5),
        "w_out": nrm((DEPTH, D_MODEL, D_MODEL), D_MODEL ** -0.5),
        "norm_ffn2": gain((DEPTH, D_MODEL)),
        "ffn2_gate": nrm((DEPTH, D_MODEL, D_FF), D_MODEL ** -0.5),
        "ffn2_up": nrm((DEPTH, D_MODEL, D_FF), D_MODEL ** -0.5),
        "ffn2_down": nrm((DEPTH, D_FF, D_MODEL), D_FF ** -0.5),
        "norm_final": gain((D_MODEL,)),
    }


def reference(x_prompt, x_sample, state_gdn, state_gdn_conv, state_ret, state_gla,
              norm_ffn1, ffn1_gate, ffn1_up, ffn1_down, norm_mix, w_in,
              gdn_conv_w, gdn_a_log, gdn_dt_bias, gdn_norm_w, gla_w_a2, gla_b_a,
              gla_norm_w, w_branch, w_out, norm_ffn2, ffn2_gate, ffn2_up, ffn2_down,
              norm_final):
    bp, lp = x_prompt.shape[0], x_prompt.shape[1]
    pos_p = jnp.arange(lp, dtype=jnp.int32)
    pos_s = PAST_LEN + jnp.arange(x_sample.shape[1], dtype=jnp.int32)
    layer_weights = (norm_ffn1, ffn1_gate, ffn1_up, ffn1_down, norm_mix, w_in,
                     gdn_conv_w, gdn_a_log, gdn_dt_bias, gdn_norm_w, gla_w_a2, gla_b_a,
                     gla_norm_w, w_branch, w_out, norm_ffn2, ffn2_gate, ffn2_up, ffn2_down)
    xp, xs = x_prompt, x_sample
    new_p = ([], [], [], [])
    new_s = ([], [], [], [])
    for i in range(DEPTH):
        lw = [a[i] for a in layer_weights]
        z_gdn = jnp.zeros((bp,) + state_gdn.shape[2:], state_gdn.dtype)
        z_conv = jnp.zeros((bp,) + state_gdn_conv.shape[2:], state_gdn_conv.dtype)
        z_ret = jnp.zeros((bp,) + state_ret.shape[2:], state_ret.dtype)
        z_gla = jnp.zeros((bp,) + state_gla.shape[2:], state_gla.dtype)
        xp, sp = decoder_layer(xp, pos_p, z_gdn, z_conv, z_ret, z_gla, *lw)
        xs, ss = decoder_layer(xs, pos_s, state_gdn[i], state_gdn_conv[i],
                               state_ret[i], state_gla[i], *lw)
        for j in range(4):
            new_p[j].append(sp[j])
            new_s[j].append(ss[j])
    y_prompt = rmsnorm(xp, norm_final)
    y_sample = rmsnorm(xs, norm_final)
    gdn_p, conv_p, ret_p, gla_p = [jnp.stack(t, axis=0) for t in new_p]
    gdn_s, conv_s, ret_s, gla_s = [jnp.stack(t, axis=0) for t in new_s]
    return (y_prompt, y_sample, gdn_p, conv_p, ret_p, gla_p, gdn_s, conv_s, ret_s, gla_s)
```

```python
import functools
import math

import jax
import jax.numpy as jnp
from jax import lax
from jax.experimental import pallas as pl
from jax.experimental.pallas import tpu as pltpu

F32 = jnp.float32
BF16 = jnp.bfloat16

D_MODEL = 4096
DEPTH = 2
PAST_LEN = 16384
D_MIX = D_MODEL // 2
GDN_DK = 128
GDN_HEADS = D_MIX // 128
GDN_DV = D_MIX // GDN_HEADS
GDN_QK = GDN_HEADS * GDN_DK
CONV_K = 4
CONV_DIM = 2 * GDN_QK + D_MIX
RET_DV = 256
RET_HEADS = D_MIX // RET_DV
RET_DK = RET_DV // 2
RET_QK = RET_HEADS * RET_DK
ROPE_BASE = 10000.0
GLA_HEADS = 4
GLA_DV = D_MIX // GLA_HEADS
GLA_DK = GLA_DV // 2
GLA_QK = GLA_HEADS * GLA_DK
GLA_RANK = 16
GLA_TAU = 16.0
CHUNK = 64
GLA_CHUNK = 16
D_FF = 11008
D_FF_PAD = 11264
EPS = 1e-6

COL_QKV_A = 0
COL_Z_A = 6144
COL_Q_B = 8192
COL_K_B = 9216
COL_V_B = 10240
COL_G_B = 12288
COL_Q_C = 14336
COL_K_C = 15360
COL_V_C = 16384
COL_R_C = 18432
COL_GATE = 20480
N_MAIN = 32768
LANE_A = 0
LANE_B = 16
LANE_LR = 32
N_SMALL = 128

LANES = 128
SUBLANES = 8
VMEM_LIMIT = 56 * 1024 * 1024


def _cparams(sem):
    return pltpu.CompilerParams(dimension_semantics=sem, vmem_limit_bytes=VMEM_LIMIT)


def _bf(x):
    return x.astype(BF16)


def _bdot(a, b):
    return jnp.dot(_bf(a), _bf(b), preferred_element_type=F32)


def _bdot_nt(a, b):
    return lax.dot_general(_bf(a), _bf(b), (((1,), (1,)), ((), ())), preferred_element_type=F32)


def _bdot_tn(a, b):
    return lax.dot_general(_bf(a), _bf(b), (((0,), (0,)), ((), ())), preferred_element_type=F32)


def _split3(x):
    hi = _bf(x)
    r1 = x - hi.astype(F32)
    mid = _bf(r1)
    lo = _bf(r1 - mid.astype(F32))
    return hi, mid, lo


def _dot01(m01, x):
    m = _bf(m01)
    hi, mid, lo = _split3(x)
    acc = jnp.dot(m, lo, preferred_element_type=F32)
    acc = acc + jnp.dot(m, mid, preferred_element_type=F32)
    return acc + jnp.dot(m, hi, preferred_element_type=F32)


def _dot01_tn(x, m01):
    m = _bf(m01)
    dn = (((0,), (0,)), ((), ()))
    hi, mid, lo = _split3(x)
    acc = lax.dot_general(lo, m, dn, preferred_element_type=F32)
    acc = acc + lax.dot_general(mid, m, dn, preferred_element_type=F32)
    return acc + lax.dot_general(hi, m, dn, preferred_element_type=F32)


def _silu(x):
    return x * jax.nn.sigmoid(x)


def _tri(c):
    ii = lax.broadcasted_iota(jnp.int32, (c, c), 0)
    jj = lax.broadcasted_iota(jnp.int32, (c, c), 1)
    return ii, jj


def _rms_kernel(x_ref, w_ref, o_ref):
    x = x_ref[...]
    y = x * lax.rsqrt(jnp.mean(x * x, axis=-1, keepdims=True) + EPS)
    o_ref[...] = (y * w_ref[...]).astype(o_ref.dtype)


def rmsnorm(x, w, out_dtype, tm=256):
    m, d = x.shape
    return pl.pallas_call(
        _rms_kernel,
        grid=(m // tm,),
        in_specs=[pl.BlockSpec((tm, d), lambda i: (i, 0)),
                  pl.BlockSpec((1, d), lambda i: (0, 0))],
        out_specs=pl.BlockSpec((tm, d), lambda i: (i, 0)),
        out_shape=jax.ShapeDtypeStruct((m, d), out_dtype),
        compiler_params=_cparams(("parallel",)),
        name="rmsnorm",
    )(x, w.reshape(1, d))


def _mm_kernel(x_ref, w_ref, o_ref):
    o_ref[...] = jnp.dot(x_ref[...], w_ref[...], preferred_element_type=F32).astype(o_ref.dtype)


def matmul(x, w, out_dtype, tm=1024, tn=1024, name="matmul"):
    m, k = x.shape
    n = w.shape[1]
    return pl.pallas_call(
        _mm_kernel,
        grid=(m // tm, n // tn),
        in_specs=[pl.BlockSpec((tm, k), lambda i, j: (i, 0)),
                  pl.BlockSpec((k, tn), lambda i, j: (0, j))],
        out_specs=pl.BlockSpec((tm, tn), lambda i, j: (i, j)),
        out_shape=jax.ShapeDtypeStruct((m, n), out_dtype),
        compiler_params=_cparams(("parallel", "parallel")),
        name=name,
    )(x, w)


def _swiglu_kernel(x_ref, wg_ref, wu_ref, o_ref):
    x = x_ref[...]
    g = jnp.dot(x, wg_ref[...], preferred_element_type=F32)
    u = jnp.dot(x, wu_ref[...], preferred_element_type=F32)
    o_ref[...] = (_silu(g) * u).astype(o_ref.dtype)


def swiglu_up(xn, wg, wu, tm=1024, tn=512):
    m, k = xn.shape
    n = wg.shape[1]
    return pl.pallas_call(
        _swiglu_kernel,
        grid=(m // tm, n // tn),
        in_specs=[pl.BlockSpec((tm, k), lambda i, j: (i, 0)),
                  pl.BlockSpec((k, tn), lambda i, j: (0, j)),
                  pl.BlockSpec((k, tn), lambda i, j: (0, j))],
        out_specs=pl.BlockSpec((tm, tn), lambda i, j: (i, j)),
        out_shape=jax.ShapeDtypeStruct((m, n), BF16),
        compiler_params=_cparams(("parallel", "parallel")),
        name="swiglu_up",
    )(xn, wg, wu)


def _mm_res_kernel(h_ref, w_ref, x_ref, o_ref, acc_ref, *, scale, nk):
    k = pl.program_id(2)
    part = jnp.dot(h_ref[...], w_ref[...], preferred_element_type=F32)

    @pl.when(k == 0)
    def _():
        acc_ref[...] = part

    @pl.when(k > 0)
    def _():
        acc_ref[...] += part

    @pl.when(k == nk - 1)
    def _():
        o_ref[...] = x_ref[...] + scale * acc_ref[...]


def matmul_residual(h, w, x, scale, tk, tm=1024, tn=1024, name="matmul_residual"):
    m, k = h.shape
    n = w.shape[1]
    nk = k // tk
    return pl.pallas_call(
        functools.partial(_mm_res_kernel, scale=scale, nk=nk),
        grid=(m // tm, n // tn, nk),
        in_specs=[pl.BlockSpec((tm, tk), lambda i, j, kk: (i, kk)),
                  pl.BlockSpec((tk, tn), lambda i, j, kk: (kk, j)),
                  pl.BlockSpec((tm, tn), lambda i, j, kk: (i, j))],
        out_specs=pl.BlockSpec((tm, tn), lambda i, j, kk: (i, j)),
        out_shape=jax.ShapeDtypeStruct((m, n), F32),
        scratch_shapes=[pltpu.VMEM((tm, tn), F32)],
        compiler_params=_cparams(("parallel", "parallel", "arbitrary")),
        name=name,
    )(h, w, x)


def _merge_kernel(o_ref, w_ref, g_ref, out_ref, acc_ref):
    r = pl.program_id(2)
    y = jnp.dot(o_ref[...], w_ref[...], preferred_element_type=F32)
    gated = jax.nn.sigmoid(g_ref[...]) * y

    @pl.when(r == 0)
    def _():
        acc_ref[...] = gated

    @pl.when(r > 0)
    def _():
        acc_ref[...] += gated

    @pl.when(r == 2)
    def _():
        out_ref[...] = acc_ref[...].astype(out_ref.dtype)


def merge_branches(o_all, w_branch, proj, tm=1024, tn=1024):
    _, m, k = o_all.shape
    n = w_branch.shape[2]
    gate_blk0 = COL_GATE // tn
    per_branch = D_MODEL // tn
    return pl.pallas_call(
        _merge_kernel,
        grid=(m // tm, n // tn, 3),
        in_specs=[pl.BlockSpec((None, tm, k), lambda i, j, r: (r, i, 0)),
                  pl.BlockSpec((None, k, tn), lambda i, j, r: (r, 0, j)),
                  pl.BlockSpec((tm, tn), lambda i, j, r: (i, gate_blk0 + r * per_branch + j))],
        out_specs=pl.BlockSpec((tm, tn), lambda i, j, r: (i, j)),
        out_shape=jax.ShapeDtypeStruct((m, n), BF16),
        scratch_shapes=[pltpu.VMEM((tm, tn), F32)],
        compiler_params=_cparams(("parallel", "parallel", "arbitrary")),
        name="merge_branches",
    )(o_all, w_branch, proj)


def _gdn_kernel(q_ref, k_ref, v_ref, z_ref, sm_ref, cwq_ref, cwk_ref, cwv_ref,
                csq_ref, csk_ref, csv_ref, hp_ref, nw_ref, s0_ref,
                o_ref, sout_ref, s_scr, xq, xk, xv, *, c, bb, hb, nchunks):
    g = pl.program_id(1)
    n = pl.program_id(2)
    pad = SUBLANES

    @pl.when(n == 0)
    def _():
        s_scr[...] = s0_ref[...]
        xq[:, 0:pad, :] = csq_ref[...]
        xk[:, 0:pad, :] = csk_ref[...]
        xv[:, 0:pad, :] = csv_ref[...]

    ii, jj = _tri(c)
    incl = ii >= jj
    strict = ii > jj
    lmat = incl.astype(F32)
    eye = (ii == jj).astype(F32)
    lane = lax.broadcasted_iota(jnp.int32, (c, LANES), 1)
    a_log = hp_ref[0:1, :]
    dt_bias = hp_ref[1:2, :]
    nw = nw_ref[...]

    def conv(x_scr, bi, cw_ref):
        y = x_scr[bi, pl.ds(pad - 3, c), :] * cw_ref[0:1, :]
        for i in range(1, CONV_K):
            y = y + x_scr[bi, pl.ds(pad - 3 + i, c), :] * cw_ref[i:i + 1, :]
        return _silu(y)

    outs = [[None] * bb for _ in range(hb)]
    for bi in range(bb):
        rows = slice(bi * c, (bi + 1) * c)
        xq[bi, pad:pad + c, :] = q_ref[rows, :]
        xk[bi, pad:pad + c, :] = k_ref[rows, :]
        xv[bi, pad:pad + c, :] = v_ref[rows, :]
        qa = conv(xq, bi, cwq_ref)
        ka = conv(xk, bi, cwk_ref)
        va = conv(xv, bi, cwv_ref)
        xq[bi, 0:pad, :] = xq[bi, c:c + pad, :]
        xk[bi, 0:pad, :] = xk[bi, c:c + pad, :]
        xv[bi, 0:pad, :] = xv[bi, c:c + pad, :]

        sm = sm_ref[rows, :]
        g_all = -jnp.exp(a_log) * jax.nn.softplus(sm + dt_bias)
        beta_all = jax.nn.sigmoid(sm)
        for hh in range(hb):
            h = g * hb + hh
            cols = slice(hh * GDN_DK, (hh + 1) * GDN_DK)
            q = qa[:, cols]
            k = ka[:, cols]
            v = va[:, cols]
            q = q * lax.rsqrt(jnp.sum(q * q, axis=-1, keepdims=True) + EPS) * (GDN_DK ** -0.5)
            k = k * lax.rsqrt(jnp.sum(k * k, axis=-1, keepdims=True) + EPS)
            g_col = jnp.sum(jnp.where(lane == h + LANE_A, g_all, 0.0), axis=-1, keepdims=True)
            beta = jnp.sum(jnp.where(lane == h + LANE_B, beta_all, 0.0), axis=-1, keepdims=True)

            gc = _dot01(lmat, jnp.broadcast_to(g_col, (c, LANES)))
            dmat = _dot01(lmat, jnp.where(strict, g_col, 0.0))
            decay = jnp.where(incl, jnp.exp(dmat), 0.0)
            eg = jnp.exp(gc)
            gl = gc[c - 1:c, :]

            kb = k * beta
            a = jnp.where(strict, _bdot_nt(kb, k) * decay, 0.0)
            x = -a
            ssum = x
            pw = x
            terms = 1
            while terms < c - 1:
                pw = _bdot(pw, pw)
                ssum = ssum + pw + _bdot(pw, ssum)
                terms = 2 * terms + 1
            t = eye + ssum
            u = _bdot(t, v * beta)
            w = _bdot(t, kb * eg)
            qk = _bdot_nt(q, k) * decay

            s_old = s_scr[bi, hh]
            v_new = u - _bdot(w, s_old)
            o = _bdot(q * eg, s_old) + _bdot(qk, v_new)
            s_scr[bi, hh] = s_old * jnp.exp(gl) + _bdot_tn(k * jnp.exp(gl - gc), v_new)

            o = o * lax.rsqrt(jnp.mean(o * o, axis=-1, keepdims=True) + EPS) * nw
            outs[hh][bi] = o * _silu(z_ref[rows, cols])

    for hh in range(hb):
        col = outs[hh][0] if bb == 1 else jnp.concatenate(outs[hh], axis=0)
        o_ref[:, hh * GDN_DV:(hh + 1) * GDN_DV] = col.astype(o_ref.dtype)

    @pl.when(n == nchunks - 1)
    def _():
        sout_ref[...] = s_scr[...]


def gdn_mixer(proj, small, conv_w, conv_state, head_params, norm_w, s0, layer,
              *, row0, nb, seq, c, bb, hb):
    nchunks = seq // c
    assert bb == 1 or nchunks == 1
    rows = bb * c
    wblk = hb * GDN_DK
    rblk0 = row0 // rows

    def row_map(col0):
        return lambda b, g, n: (rblk0 + b * nchunks + n, col0 // wblk + g)

    def cw_map(col0):
        return lambda b, g, n: (0, col0 // wblk + g)

    def cs_map(col0):
        return lambda b, g, n: (layer, b, 0, col0 // wblk + g)

    qkv_cols = (COL_QKV_A, COL_QKV_A + GDN_QK, COL_QKV_A + 2 * GDN_QK)
    in_specs = (
        [pl.BlockSpec((rows, wblk), row_map(c0)) for c0 in qkv_cols]
        + [pl.BlockSpec((rows, wblk), row_map(COL_Z_A)),
           pl.BlockSpec((rows, N_SMALL), lambda b, g, n: (rblk0 + b * nchunks + n, 0))]
        + [pl.BlockSpec((CONV_K, wblk), cw_map(c0)) for c0 in qkv_cols]
        + [pl.BlockSpec((None, bb, SUBLANES, wblk), cs_map(c0)) for c0 in qkv_cols]
        + [pl.BlockSpec((SUBLANES, LANES), lambda b, g, n: (0, 0)),
           pl.BlockSpec((1, GDN_DV), lambda b, g, n: (0, 0)),
           pl.BlockSpec((None, bb, hb, GDN_DK, GDN_DV), lambda b, g, n: (layer, b, g, 0, 0))])
    out_specs = [pl.BlockSpec((rows, wblk), lambda b, g, n: (b * nchunks + n, g)),
                 pl.BlockSpec((bb, hb, GDN_DK, GDN_DV), lambda b, g, n: (b, g, 0, 0))]
    out_shape = [jax.ShapeDtypeStruct((nb * seq, D_MIX), BF16),
                 jax.ShapeDtypeStruct((nb, GDN_HEADS, GDN_DK, GDN_DV), F32)]
    return pl.pallas_call(
        functools.partial(_gdn_kernel, c=c, bb=bb, hb=hb, nchunks=nchunks),
        grid=(nb // bb, GDN_HEADS // hb, nchunks),
        in_specs=in_specs,
        out_specs=out_specs,
        out_shape=out_shape,
        scratch_shapes=[pltpu.VMEM((bb, hb, GDN_DK, GDN_DV), F32)]
        + [pltpu.VMEM((bb, SUBLANES + c, wblk), F32)] * 3,
        compiler_params=_cparams(("parallel", "parallel", "arbitrary")),
        name="gdn_mixer",
    )(proj, proj, proj, proj, small, conv_w, conv_w, conv_w,
      conv_state, conv_state, conv_state, head_params, norm_w, s0)


def _ret_kernel(q_ref, k_ref, v_ref, gt_ref, tab_ref, lg_ref, s0_ref,
                o_ref, sout_ref, s_scr, *, c, bb, hb, nchunks):
    n = pl.program_id(2)

    @pl.when(n == 0)
    def _():
        s_scr[...] = s0_ref[...]

    ii, jj = _tri(c)
    incl = ii >= jj
    dpos = (ii - jj).astype(F32)
    trow = lax.broadcasted_iota(jnp.int32, (c, RET_DK), 0).astype(F32)
    cos = tab_ref[0]
    sin = tab_ref[1]

    def rope(x):
        return x * cos + pltpu.roll(x, RET_DK // 2, axis=1) * sin

    outs = [[None] * bb for _ in range(hb)]
    for hh in range(hb):
        lg = lg_ref[hh, 0:1, :]
        lg_k = lg[:, 0:RET_DK]
        decay = jnp.where(incl, jnp.exp(dpos * lg[:, 0:c]), 0.0)
        q_scale = jnp.exp((trow + 1.0) * lg_k)
        k_scale = jnp.exp((c - 1.0 - trow) * lg_k)
        chunk_decay = jnp.exp(c * lg)
        for bi in range(bb):
            rows = slice(bi * c, (bi + 1) * c)
            q = rope(q_ref[rows, hh * RET_DK:(hh + 1) * RET_DK])
            k = rope(k_ref[rows, hh * RET_DK:(hh + 1) * RET_DK]) * (RET_DK ** -0.5)
            v = v_ref[rows, hh * RET_DV:(hh + 1) * RET_DV]
            scores = _bdot_nt(q, k) * decay
            s_old = s_scr[bi, hh]
            o = _bdot(q * q_scale, s_old) + _bdot(scores, v)
            s_scr[bi, hh] = s_old * chunk_decay + _bdot_tn(k * k_scale, v)
            mu = jnp.mean(o, axis=-1, keepdims=True)
            var = jnp.mean(jnp.square(o - mu), axis=-1, keepdims=True)
            on = (o - mu) * lax.rsqrt(var + 1e-5)
            outs[hh][bi] = on * _silu(gt_ref[rows, hh * RET_DV:(hh + 1) * RET_DV])

    for hh in range(hb):
        col = outs[hh][0] if bb == 1 else jnp.concatenate(outs[hh], axis=0)
        o_ref[:, hh * RET_DV:(hh + 1) * RET_DV] = col.astype(o_ref.dtype)

    @pl.when(n == nchunks - 1)
    def _():
        sout_ref[...] = s_scr[...]


def ret_mixer(proj, rope_tab, lg_tab, s0, layer, *, row0, nb, seq, c, bb, hb):
    nchunks = seq // c
    assert bb == 1 or nchunks == 1
    rows = bb * c
    wk = hb * RET_DK
    wv = hb * RET_DV
    rblk0 = row0 // rows

    def row_map(col0, w):
        return lambda b, g, n: (rblk0 + b * nchunks + n, col0 // w + g)

    in_specs = [pl.BlockSpec((rows, wk), row_map(COL_Q_B, wk)),
                pl.BlockSpec((rows, wk), row_map(COL_K_B, wk)),
                pl.BlockSpec((rows, wv), row_map(COL_V_B, wv)),
                pl.BlockSpec((rows, wv), row_map(COL_G_B, wv)),
                pl.BlockSpec((2, c, RET_DK), lambda b, g, n: (0, n, 0)),
                pl.BlockSpec((hb, SUBLANES, RET_DV), lambda b, g, n: (g, 0, 0)),
                pl.BlockSpec((None, bb, hb, RET_DK, RET_DV), lambda b, g, n: (layer, b, g, 0, 0))]
    out_specs = [pl.BlockSpec((rows, wv), lambda b, g, n: (b * nchunks + n, g)),
                 pl.BlockSpec((bb, hb, RET_DK, RET_DV), lambda b, g, n: (b, g, 0, 0))]
    out_shape = [jax.ShapeDtypeStruct((nb * seq, D_MIX), BF16),
                 jax.ShapeDtypeStruct((nb, RET_HEADS, RET_DK, RET_DV), F32)]
    return pl.pallas_call(
        functools.partial(_ret_kernel, c=c, bb=bb, hb=hb, nchunks=nchunks),
        grid=(nb // bb, RET_HEADS // hb, nchunks),
        in_specs=in_specs,
        out_specs=out_specs,
        out_shape=out_shape,
        scratch_shapes=[pltpu.VMEM((bb, hb, RET_DK, RET_DV), F32)],
        compiler_params=_cparams(("parallel", "parallel", "arbitrary")),
        name="ret_mixer",
    )(proj, proj, proj, proj, rope_tab, lg_tab, s0)


def _gla_kernel(q_ref, k_ref, v_ref, r_ref, sm_ref, wa_ref, ba_ref, nw_ref, s0_ref,
                o_ref, sout_ref, s_scr, *, c, bb, hb, nchunks):
    n = pl.program_id(2)

    @pl.when(n == 0)
    def _():
        s_scr[...] = s0_ref[...]

    ii, jj = _tri(c)
    incl = ii >= jj
    lmat = incl.astype(F32)
    ones = jnp.ones((c, LANES), F32)
    nw = nw_ref[...]

    outs = [[None] * bb for _ in range(hb)]
    for bi in range(bb):
        rows = slice(bi * c, (bi + 1) * c)
        sm = sm_ref[rows, :]
        for hh in range(hb):
            kc = slice(hh * GLA_DK, (hh + 1) * GLA_DK)
            vc = slice(hh * GLA_DV, (hh + 1) * GLA_DV)
            pre = jnp.dot(_bf(sm), wa_ref[:, kc], preferred_element_type=F32) + ba_ref[:, kc]
            log_a = jax.nn.log_sigmoid(pre) / GLA_TAU
            bcum = _dot01(lmat, log_a)
            b_last = bcum[c - 1:c, :]
            q = q_ref[rows, kc] * (GLA_DK ** -0.5)
            k = k_ref[rows, kc]
            v = v_ref[rows, vc]
            qd = q * jnp.exp(bcum)
            kd = k * jnp.exp(-bcum)
            attn = jnp.where(incl, _bdot_nt(qd, kd), 0.0)
            kl = k * jnp.exp(b_last - bcum)
            a_col = jnp.exp(_dot01_tn(log_a, ones))
            a_full = jnp.concatenate([a_col] * (GLA_DV // LANES), axis=1)
            s_old = s_scr[bi, hh]
            o = _bdot(qd, s_old) + _bdot(attn, v)
            s_scr[bi, hh] = s_old * a_full + _bdot_tn(kl, v)
            on = o * lax.rsqrt(jnp.mean(o * o, axis=-1, keepdims=True) + EPS) * nw
            outs[hh][bi] = on * _silu(r_ref[rows, vc])

    for hh in range(hb):
        col = outs[hh][0] if bb == 1 else jnp.concatenate(outs[hh], axis=0)
        o_ref[:, hh * GLA_DV:(hh + 1) * GLA_DV] = col.astype(o_ref.dtype)

    @pl.when(n == nchunks - 1)
    def _():
        sout_ref[...] = s_scr[...]


def gla_mixer(proj, small, wa_pad, b_a, norm_w, s0, layer, *, row0, nb, seq, c, bb, hb):
    nchunks = seq // c
    assert bb == 1 or nchunks == 1
    rows = bb * c
    wk = hb * GLA_DK
    wv = hb * GLA_DV
    rblk0 = row0 // rows

    def row_map(col0, w):
        return lambda b, g, n: (rblk0 + b * nchunks + n, col0 // w + g)

    in_specs = [pl.BlockSpec((rows, wk), row_map(COL_Q_C, wk)),
                pl.BlockSpec((rows, wk), row_map(COL_K_C, wk)),
                pl.BlockSpec((rows, wv), row_map(COL_V_C, wv)),
                pl.BlockSpec((rows, wv), row_map(COL_R_C, wv)),
                pl.BlockSpec((rows, N_SMALL), lambda b, g, n: (rblk0 + b * nchunks + n, 0)),
                pl.BlockSpec((N_SMALL, wk), lambda b, g, n: (0, g)),
                pl.BlockSpec((1, wk), lambda b, g, n: (0, g)),
                pl.BlockSpec((1, GLA_DV), lambda b, g, n: (0, 0)),
                pl.BlockSpec((None, bb, hb, GLA_DK, GLA_DV), lambda b, g, n: (layer, b, g, 0, 0))]
    out_specs = [pl.BlockSpec((rows, wv), lambda b, g, n: (b * nchunks + n, g)),
                 pl.BlockSpec((bb, hb, GLA_DK, GLA_DV), lambda b, g, n: (b, g, 0, 0))]
    out_shape = [jax.ShapeDtypeStruct((nb * seq, D_MIX), BF16),
                 jax.ShapeDtypeStruct((nb, GLA_HEADS, GLA_DK, GLA_DV), F32)]
    return pl.pallas_call(
        functools.partial(_gla_kernel, c=c, bb=bb, hb=hb, nchunks=nchunks),
        grid=(nb // bb, GLA_HEADS // hb, nchunks),
        in_specs=in_specs,
        out_specs=out_specs,
        out_shape=out_shape,
        scratch_shapes=[pltpu.VMEM((bb, hb, GLA_DK, GLA_DV), F32)],
        compiler_params=_cparams(("parallel", "parallel", "arbitrary")),
        name="gla_mixer",
    )(proj, proj, proj, proj, small, wa_pad, b_a, norm_w, s0)


def _rope_table(pos):
    half = RET_DK // 2
    inv = ROPE_BASE ** (-jnp.arange(half, dtype=F32) / half)
    ang = pos.astype(F32)[:, None] * inv[None, :]
    cos, sin = jnp.cos(ang), jnp.sin(ang)
    return jnp.stack([jnp.concatenate([cos, cos], axis=-1),
                      jnp.concatenate([-sin, sin], axis=-1)], axis=0)


def _reorder_w_in(w):
    a0 = CONV_DIM + D_MIX
    b0 = a0 + 2 * GDN_HEADS
    lr0 = b0 + 2 * RET_QK + 2 * D_MIX + 2 * GLA_QK + 2 * D_MIX
    g0 = lr0 + GLA_RANK
    main = jnp.concatenate([w[:, :a0], w[:, b0:lr0], w[:, g0:]], axis=1)
    small = jnp.concatenate(
        [w[:, a0:b0], w[:, lr0:g0],
         jnp.zeros((w.shape[0], N_SMALL - 2 * GDN_HEADS - GLA_RANK), w.dtype)], axis=1)
    return _bf(main), _bf(small)


def _pad_ff(w, axis):
    pad = [(0, 0)] * w.ndim
    pad[axis] = (0, D_FF_PAD - D_FF)
    return jnp.pad(_bf(w), pad)


def _ffn(x, norm_w, wg, wu, wd):
    xn = rmsnorm(x, norm_w, BF16)
    h = swiglu_up(xn, _pad_ff(wg, 1), _pad_ff(wu, 1))
    return matmul_residual(h, _pad_ff(wd, 0), x, 0.5, tk=D_FF_PAD // 4, name="ffn_down")


def kernel(x_prompt, x_sample, state_gdn, state_gdn_conv, state_ret, state_gla, norm_ffn1, ffn1_gate, ffn1_up, ffn1_down, norm_mix, w_in, gdn_conv_w, gdn_a_log, gdn_dt_bias, gdn_norm_w, gla_w_a2, gla_b_a, gla_norm_w, w_branch, w_out, norm_ffn2, ffn2_gate, ffn2_up, ffn2_down, norm_final):
    bp, lp, _ = x_prompt.shape
    bs, ls, _ = x_sample.shape
    mp = bp * lp
    ms = bs * ls
    x = jnp.concatenate([x_prompt.reshape(mp, D_MODEL), x_sample.reshape(ms, D_MODEL)], axis=0)

    rope_p = _rope_table(jnp.arange(lp, dtype=jnp.int32))
    rope_s = _rope_table(PAST_LEN + jnp.arange(ls, dtype=jnp.int32))
    log_gamma = jnp.log1p(-jnp.exp2(-5.0 - jnp.arange(RET_HEADS, dtype=F32)))
    lg_tab = jnp.broadcast_to(log_gamma[:, None, None], (RET_HEADS, SUBLANES, RET_DV))

    zero_gdn = jnp.zeros((1, bp, GDN_HEADS, GDN_DK, GDN_DV), F32)
    zero_ret = jnp.zeros((1, bp, RET_HEADS, RET_DK, RET_DV), F32)
    zero_gla = jnp.zeros((1, bp, GLA_HEADS, GLA_DK, GLA_DV), F32)
    zero_conv = jnp.zeros((1, bp, SUBLANES, CONV_DIM), F32)
    conv_s_pad = jnp.pad(state_gdn_conv, ((0, 0), (0, 0), (SUBLANES - (CONV_K - 1), 0), (0, 0)))

    cp = math.gcd(lp, CHUNK)
    cs = math.gcd(ls, CHUNK)
    gcp = math.gcd(lp, GLA_CHUNK)
    gcs = math.gcd(ls, GLA_CHUNK)

    new_p = ([], [], [], [])
    new_s = ([], [], [], [])
    for i in range(DEPTH):
        x = _ffn(x, norm_ffn1[i], ffn1_gate[i], ffn1_up[i], ffn1_down[i])

        xn = rmsnorm(x, norm_mix[i], BF16)
        w_main, w_small = _reorder_w_in(w_in[i])
        proj = matmul(xn, w_main, F32, name="in_proj")
        small = matmul(xn, w_small, F32, tn=N_SMALL, name="in_proj_small")

        head_params = jnp.zeros((SUBLANES, LANES), F32)
        head_params = head_params.at[0, :GDN_HEADS].set(gdn_a_log[i])
        head_params = head_params.at[1, :GDN_HEADS].set(gdn_dt_bias[i])
        gnw = gdn_norm_w[i].reshape(1, GDN_DV)
        wa_pad = jnp.zeros((N_SMALL, GLA_QK), F32).at[LANE_LR:LANE_LR + GLA_RANK].set(gla_w_a2[i])
        wa_pad = _bf(wa_pad)
        b_a = gla_b_a[i].reshape(1, GLA_QK)
        lnw = gla_norm_w[i].reshape(1, GLA_DV)

        oa_p, gdn_p = gdn_mixer(proj, small, gdn_conv_w[i], zero_conv, head_params, gnw, zero_gdn, 0,
                                row0=0, nb=bp, seq=lp, c=cp, bb=1, hb=4)
        oa_s, gdn_s = gdn_mixer(proj, small, gdn_conv_w[i], conv_s_pad, head_params, gnw, state_gdn, i,
                                row0=mp, nb=bs, seq=ls, c=cs, bb=2, hb=4)
        ob_p, ret_p = ret_mixer(proj, rope_p, lg_tab, zero_ret, 0,
                                row0=0, nb=bp, seq=lp, c=cp, bb=1, hb=4)
        ob_s, ret_s = ret_mixer(proj, rope_s, lg_tab, state_ret, i,
                                row0=mp, nb=bs, seq=ls, c=cs, bb=2, hb=4)
        oc_p, gla_p = gla_mixer(proj, small, wa_pad, b_a, lnw, zero_gla, 0,
                                row0=0, nb=bp, seq=lp, c=gcp, bb=1, hb=2)
        oc_s, gla_s = gla_mixer(proj, small, wa_pad, b_a, lnw, state_gla, i,
                                row0=mp, nb=bs, seq=ls, c=gcs, bb=2, hb=2)

        qkv_p = proj[:mp, :CONV_DIM].reshape(bp, lp, CONV_DIM)
        qkv_s = proj[mp:, :CONV_DIM].reshape(bs, ls, CONV_DIM)
        conv_p = qkv_p[:, lp - (CONV_K - 1):, :]
        conv_s = jnp.concatenate([state_gdn_conv[i], qkv_s], axis=1)[:, ls:, :]

        o_all = jnp.stack([jnp.concatenate([oa_p, oa_s], axis=0),
                           jnp.concatenate([ob_p, ob_s], axis=0),
                           jnp.concatenate([oc_p, oc_s], axis=0)], axis=0)
        merged = merge_branches(o_all, _bf(w_branch[i]), proj)
        x = matmul_residual(merged, _bf(w_out[i]), x, 1.0, tk=D_MODEL, name="out_proj")

        x = _ffn(x, norm_ffn2[i], ffn2_gate[i], ffn2_up[i], ffn2_down[i])

        for lst, val in zip(new_p, (gdn_p, conv_p, ret_p, gla_p)):
            lst.append(val)
        for lst, val in zip(new_s, (gdn_s, conv_s, ret_s, gla_s)):
            lst.append(val)

    y = rmsnorm(x, norm_final, F32)
    y_prompt = y[:mp].reshape(bp, lp, D_MODEL)
    y_sample = y[mp:].reshape(bs, ls, D_MODEL)
    outs_p = [jnp.stack(t, axis=0) for t in new_p]
    outs_s = [jnp.stack(t, axis=0) for t in new_s]
    return (y_prompt, y_sample, *outs_p, *outs_s)
```

```python
import functools
import math

import jax
import jax.numpy as jnp
from jax import lax
from jax.experimental import pallas as pl
from jax.experimental.pallas import tpu as pltpu

F32 = jnp.float32
BF16 = jnp.bfloat16

D_MODEL = 4096
DEPTH = 2
PAST_LEN = 16384
N_BRANCH = 3
D_MIX = D_MODEL // 2
GDN_DK = 128
GDN_HEADS = D_MIX // 128
GDN_DV = D_MIX // GDN_HEADS
GDN_QK = GDN_HEADS * GDN_DK
CONV_K = 4
CONV_DIM = 2 * GDN_QK + D_MIX
RET_DV = 256
RET_HEADS = D_MIX // RET_DV
RET_DK = RET_DV // 2
RET_QK = RET_HEADS * RET_DK
ROPE_BASE = 10000.0
GLA_HEADS = 4
GLA_DV = D_MIX // GLA_HEADS
GLA_DK = GLA_DV // 2
GLA_QK = GLA_HEADS * GLA_DK
GLA_RANK = 16
GLA_TAU = 16.0
CHUNK = 64
GLA_CHUNK = 16
D_FF = 11008
EPS = 1e-6

COL_QKV_A = 0
COL_Z_A = 6144
COL_Q_B = 8192
COL_K_B = 9216
COL_V_B = 10240
COL_G_B = 12288
COL_Q_C = 14336
COL_K_C = 15360
COL_V_C = 16384
COL_R_C = 18432
COL_GATE = 20480
N_MAIN = 32768
LANE_A = 0
LANE_B = 16
LANE_LR = 32
N_SMALL = 128

LANES = 128
SUBLANES = 8
MXU_DIM = 256
VMEM_LIMIT = 56 * 1024 * 1024

TM = 1024
TN_PROJ = 1024
TN_UP = MXU_DIM
TN_DOWN = 512
TK_DOWN = 11 * MXU_DIM
TM_NORM = 256
MIXER_CFG_PROMPT = {"gdn": dict(bb=1, hb=8), "ret": dict(bb=2, hb=8), "gla": dict(bb=4, hb=4)}
MIXER_CFG_SAMPLE = {"gdn": dict(bb=4, hb=4), "ret": dict(bb=4, hb=4), "gla": dict(bb=2, hb=4)}


def _cparams(sem):
    return pltpu.CompilerParams(dimension_semantics=sem, vmem_limit_bytes=VMEM_LIMIT)


def _bf(x):
    return x.astype(BF16)


def _bdot(a, b):
    return jnp.dot(_bf(a), _bf(b), preferred_element_type=F32)


def _bdot_nt(a, b):
    return lax.dot_general(_bf(a), _bf(b), (((1,), (1,)), ((), ())), preferred_element_type=F32)


def _bdot_tn(a, b):
    return lax.dot_general(_bf(a), _bf(b), (((0,), (0,)), ((), ())), preferred_element_type=F32)


def _split3(x):
    hi = _bf(x)
    r1 = x - hi.astype(F32)
    mid = _bf(r1)
    lo = _bf(r1 - mid.astype(F32))
    return hi, mid, lo


def _dot01(m01, x):
    m = _bf(m01)
    hi, mid, lo = _split3(x)
    acc = jnp.dot(m, lo, preferred_element_type=F32)
    acc = acc + jnp.dot(m, mid, preferred_element_type=F32)
    return acc + jnp.dot(m, hi, preferred_element_type=F32)


def _dot01_tn(x, m01):
    m = _bf(m01)
    dn = (((0,), (0,)), ((), ()))
    hi, mid, lo = _split3(x)
    acc = lax.dot_general(lo, m, dn, preferred_element_type=F32)
    acc = acc + lax.dot_general(mid, m, dn, preferred_element_type=F32)
    return acc + lax.dot_general(hi, m, dn, preferred_element_type=F32)


def _silu(x):
    return x * jax.nn.sigmoid(x)


def _iota2(shape):
    return (lax.broadcasted_iota(jnp.int32, shape, 0), lax.broadcasted_iota(jnp.int32, shape, 1))


def _rms_kernel(x_ref, w_ref, o_ref):
    x = x_ref[...]
    y = x * lax.rsqrt(jnp.mean(x * x, axis=-1, keepdims=True) + EPS)
    o_ref[...] = (y * w_ref[...]).astype(o_ref.dtype)


def rmsnorm(x, w, out_dtype):
    m, d = x.shape
    return pl.pallas_call(
        _rms_kernel,
        grid=(m // TM_NORM,),
        in_specs=[pl.BlockSpec((TM_NORM, d), lambda i: (i, 0)),
                  pl.BlockSpec((1, d), lambda i: (0, 0))],
        out_specs=pl.BlockSpec((TM_NORM, d), lambda i: (i, 0)),
        out_shape=jax.ShapeDtypeStruct((m, d), out_dtype),
        compiler_params=_cparams(("parallel",)),
        name="rmsnorm",
    )(x, w.reshape(1, d))


def _mm_kernel(x_ref, w_ref, o_ref):
    o_ref[...] = jnp.dot(x_ref[...], w_ref[...], preferred_element_type=F32).astype(o_ref.dtype)


def matmul(x, w, out_dtype, tn, name):
    m, k = x.shape
    n = w.shape[1]
    return pl.pallas_call(
        _mm_kernel,
        grid=(m // TM, n // tn),
        in_specs=[pl.BlockSpec((TM, k), lambda i, j: (i, 0)),
                  pl.BlockSpec((k, tn), lambda i, j: (0, j))],
        out_specs=pl.BlockSpec((TM, tn), lambda i, j: (i, j)),
        out_shape=jax.ShapeDtypeStruct((m, n), out_dtype),
        compiler_params=_cparams(("parallel", "parallel")),
        name=name,
    )(x, w)


def _swiglu_kernel(x_ref, wg_ref, wu_ref, o_ref, wg_bf, wu_bf):
    @pl.when(pl.program_id(1) == 0)
    def _():
        wg_bf[...] = _bf(wg_ref[...])
        wu_bf[...] = _bf(wu_ref[...])

    x = x_ref[...]
    g = jnp.dot(x, wg_bf[...], preferred_element_type=F32)
    u = jnp.dot(x, wu_bf[...], preferred_element_type=F32)
    o_ref[...] = (_silu(g) * u).astype(o_ref.dtype)


def swiglu_up(xn, wg, wu):
    m, k = xn.shape
    n = wg.shape[1]
    return pl.pallas_call(
        _swiglu_kernel,
        grid=(n // TN_UP, m // TM),
        in_specs=[pl.BlockSpec((TM, k), lambda j, i: (i, 0)),
                  pl.BlockSpec((k, TN_UP), lambda j, i: (0, j)),
                  pl.BlockSpec((k, TN_UP), lambda j, i: (0, j))],
        out_specs=pl.BlockSpec((TM, TN_UP), lambda j, i: (i, j)),
        out_shape=jax.ShapeDtypeStruct((m, n), BF16),
        scratch_shapes=[pltpu.VMEM((k, TN_UP), BF16)] * 2,
        compiler_params=_cparams(("arbitrary", "arbitrary")),
        name="swiglu_up",
    )(xn, wg, wu)


def _mm_res_kernel(h_ref, w_ref, x_ref, o_ref, acc_ref, *, scale, nk, k_last):
    kk = pl.program_id(2)
    tk = h_ref.shape[1]

    def part(kv):
        return jnp.dot(h_ref[:, :kv], _bf(w_ref[:kv, :]), preferred_element_type=F32)

    if nk == 1:
        o_ref[...] = x_ref[...] + scale * part(k_last)
        return

    @pl.when(kk == 0)
    def _():
        acc_ref[...] = part(tk)

    @pl.when(jnp.logical_and(kk > 0, kk < nk - 1))
    def _():
        acc_ref[...] += part(tk)

    @pl.when(kk == nk - 1)
    def _():
        o_ref[...] = x_ref[...] + scale * (acc_ref[...] + part(k_last))


def matmul_residual(h, w, x, scale, tk, name):
    m, k = h.shape
    n = w.shape[1]
    nk = pl.cdiv(k, tk)
    k_last = k - (nk - 1) * tk
    return pl.pallas_call(
        functools.partial(_mm_res_kernel, scale=scale, nk=nk, k_last=k_last),
        grid=(m // TM, n // TN_DOWN, nk),
        in_specs=[pl.BlockSpec((TM, tk), lambda i, j, kk: (i, kk)),
                  pl.BlockSpec((tk, TN_DOWN), lambda i, j, kk: (kk, j)),
                  pl.BlockSpec((TM, TN_DOWN), lambda i, j, kk: (i, j))],
        out_specs=pl.BlockSpec((TM, TN_DOWN), lambda i, j, kk: (i, j)),
        out_shape=jax.ShapeDtypeStruct((m, n), F32),
        scratch_shapes=[pltpu.VMEM((TM, TN_DOWN), F32)],
        compiler_params=_cparams(("parallel", "parallel", "arbitrary")),
        name=name,
    )(h, w, x)


def _merge_kernel(o_ref, w_ref, g_ref, out_ref, acc_ref):
    r = pl.program_id(2)
    y = jnp.dot(o_ref[...], _bf(w_ref[...]), preferred_element_type=F32)
    gated = jax.nn.sigmoid(g_ref[...]) * y

    @pl.when(r == 0)
    def _():
        acc_ref[...] = gated

    @pl.when(r > 0)
    def _():
        acc_ref[...] += gated

    @pl.when(r == N_BRANCH - 1)
    def _():
        out_ref[...] = acc_ref[...].astype(out_ref.dtype)


def merge_branches(o_all, w_branch, proj):
    _, m, k = o_all.shape
    n = w_branch.shape[2]
    gate_blk0 = COL_GATE // TN_PROJ
    per_branch = D_MODEL // TN_PROJ
    return pl.pallas_call(
        _merge_kernel,
        grid=(m // TM, n // TN_PROJ, N_BRANCH),
        in_specs=[pl.BlockSpec((None, TM, k), lambda i, j, r: (r, i, 0)),
                  pl.BlockSpec((None, k, TN_PROJ), lambda i, j, r: (r, 0, j)),
                  pl.BlockSpec((TM, TN_PROJ), lambda i, j, r: (i, gate_blk0 + r * per_branch + j))],
        out_specs=pl.BlockSpec((TM, TN_PROJ), lambda i, j, r: (i, j)),
        out_shape=jax.ShapeDtypeStruct((m, n), BF16),
        scratch_shapes=[pltpu.VMEM((TM, TN_PROJ), F32)],
        compiler_params=_cparams(("parallel", "parallel", "arbitrary")),
        name="merge_branches",
    )(o_all, w_branch, proj)


def _mixer_call(body, name, grid, in_specs, operands, scratch, *, branch, rows, wv, rblk0,
                o_buf, st_buf, out_layer, m_total, nb, heads, dk, dv, ns, hb):
    steps = grid[2]
    out_specs = [pl.BlockSpec((None, rows, wv), lambda b, g, n: (branch, rblk0 + b * steps + n, g)),
                 pl.BlockSpec((None, ns, hb, dk, dv), lambda b, g, n: (out_layer, b, g, 0, 0))]
    out_shape = [jax.ShapeDtypeStruct((N_BRANCH, m_total, D_MIX), BF16),
                 jax.ShapeDtypeStruct((DEPTH, nb, heads, dk, dv), F32)]
    aliases = {}
    operands = list(operands)
    in_specs = list(in_specs)
    n_alias = 0
    for out_idx, buf in ((0, o_buf), (1, st_buf)):
        if buf is not None:
            aliases[len(operands)] = out_idx
            operands.append(buf)
            in_specs.append(pl.BlockSpec(memory_space=pl.ANY))
            n_alias += 1

    def wrapped(*refs):
        n_in = len(operands)
        body(*refs[:n_in - n_alias], *refs[n_in:])

    return pl.pallas_call(
        wrapped,
        grid=grid,
        in_specs=in_specs,
        out_specs=out_specs,
        out_shape=out_shape,
        scratch_shapes=scratch,
        input_output_aliases=aliases,
        compiler_params=_cparams(("parallel", "parallel", "arbitrary")),
        name=name,
    )(*operands)


def _mixer_grid(row0, nb, seq, c, bb, head_groups):
    nchunks = seq // c
    chained = nchunks > 1
    if chained:
        assert nchunks % bb == 0
        grid = (nb, head_groups, nchunks // bb)
    else:
        assert nb % bb == 0
        grid = (nb // bb, head_groups, 1)
    rows = bb * c
    assert row0 % rows == 0
    rblk0 = row0 // rows
    steps = grid[2]

    def row_map(col0, w):
        return lambda b, g, n: (rblk0 + b * steps + n, col0 // w + g)

    return grid, chained, (1 if chained else bb), rows, rblk0, row_map


def _gdn_kernel(q_ref, k_ref, v_ref, z_ref, sm_ref, cwq_ref, cwk_ref, cwv_ref,
                csq_ref, csk_ref, csv_ref, hp_ref, nw_ref, s0_ref,
                o_ref, sout_ref, s_scr, xq, xk, xv, gct_scr, *, c, bb, hb, nchunks):
    g = pl.program_id(1)
    n = pl.program_id(2)
    pad = SUBLANES
    w2 = 2 * c

    @pl.when(n == 0)
    def _():
        s_scr[...] = s0_ref[...]
        xq[:, 0:pad, :] = csq_ref[...]
        xk[:, 0:pad, :] = csk_ref[...]
        xv[:, 0:pad, :] = csv_ref[...]

    ii, jw = _iota2((c, w2))
    jm = jnp.where(jw >= c, jw - c, jw)
    left = jw < c
    incl_w = ii >= jm
    strict_w = ii > jm
    eye_right = (jw == ii + c).astype(F32)
    cum_w = (ii <= jm).astype(F32)
    ic, jc = _iota2((c, c))
    lmat = (ic >= jc).astype(F32)
    lane = lax.broadcasted_iota(jnp.int32, (c, LANES), 1)
    a_log = hp_ref[0:1, :]
    dt_bias = hp_ref[1:2, :]
    nw = nw_ref[...]

    def conv(x_scr, bi, cw_ref):
        y = x_scr[bi, pl.ds(pad - 3, c), :] * cw_ref[0:1, :]
        for i in range(1, CONV_K):
            y = y + x_scr[bi, pl.ds(pad - 3 + i, c), :] * cw_ref[i:i + 1, :]
        return _silu(y)

    units = [(bi, hh) for bi in range(bb) for hh in range(hb)]
    st = {}
    for bi in range(bb):
        rows = slice(bi * c, (bi + 1) * c)
        xq[bi, pad:pad + c, :] = q_ref[rows, :]
        xk[bi, pad:pad + c, :] = k_ref[rows, :]
        xv[bi, pad:pad + c, :] = v_ref[rows, :]
        qa = conv(xq, bi, cwq_ref)
        ka = conv(xk, bi, cwk_ref)
        va = conv(xv, bi, cwv_ref)
        xq[bi, 0:pad, :] = xq[bi, c:c + pad, :]
        xk[bi, 0:pad, :] = xk[bi, c:c + pad, :]
        xv[bi, 0:pad, :] = xv[bi, c:c + pad, :]

        sm = sm_ref[rows, :]
        g_all = -jnp.exp(a_log) * jax.nn.softplus(sm + dt_bias)
        beta_all = jax.nn.sigmoid(sm)
        gc_all = _dot01(lmat, g_all)
        gct_scr[bi] = _dot01_tn(g_all, cum_w)
        for hh in range(hb):
            h = g * hb + hh
            cols = slice(hh * GDN_DK, (hh + 1) * GDN_DK)
            q = qa[:, cols]
            k = ka[:, cols]
            q = q * lax.rsqrt(jnp.sum(q * q, axis=-1, keepdims=True) + EPS) * (GDN_DK ** -0.5)
            k = k * lax.rsqrt(jnp.sum(k * k, axis=-1, keepdims=True) + EPS)
            gc_col = jnp.sum(jnp.where(lane == h + LANE_A, gc_all, 0.0), axis=-1, keepdims=True)
            beta = jnp.sum(jnp.where(lane == h + LANE_B, beta_all, 0.0), axis=-1, keepdims=True)
            gc = jnp.broadcast_to(gc_col, (c, GDN_DK))
            st[bi, hh] = dict(q=q, k=k, v=va[:, cols], beta=beta, gc_col=gc_col, gc=gc, kb=k * beta)

    for bi, hh in units:
        u_ = st[bi, hh]
        gc_row = gct_scr[bi, pl.ds(g * hb + hh, 1), :]
        decay_w = jnp.where(incl_w, jnp.exp(u_["gc_col"] - gc_row), 0.0)
        r = _bdot_nt(jnp.concatenate([u_["kb"], u_["q"]], axis=0),
                     jnp.concatenate([u_["k"], u_["k"]], axis=0))
        top = r[0:c, :] * decay_w
        u_["qk"] = r[c:w2, 0:c] * decay_w[:, 0:c]
        u_["wmat"] = jnp.where(left, jnp.where(strict_w, -top, 0.0), eye_right)

    for _ in range(c.bit_length() - 1):
        for unit in units:
            wmat = st[unit]["wmat"]
            st[unit]["wmat"] = _bdot(wmat[:, 0:c], wmat) + jnp.where(left, 0.0, wmat)

    for unit in units:
        u_ = st[unit]
        eg = jnp.exp(u_["gc"])
        u_["qe"] = u_["q"] * eg
        u_["uw"] = _bdot(u_["wmat"][:, c:w2],
                         jnp.concatenate([u_["v"] * u_["beta"], u_["kb"] * eg], axis=1))

    for bi, hh in units:
        u_ = st[bi, hh]
        u_["s_old"] = s_scr[bi, hh]
        u_["both"] = _bdot(jnp.concatenate([u_["uw"][:, GDN_DV:], u_["qe"]], axis=0), u_["s_old"])

    for unit in units:
        u_ = st[unit]
        u_["v_new"] = u_["uw"][:, 0:GDN_DV] - u_["both"][0:c, :]
        u_["o"] = u_["both"][c:w2, :] + _bdot(u_["qk"], u_["v_new"])

    outs = [[None] * bb for _ in range(hb)]
    for bi, hh in units:
        u_ = st[bi, hh]
        gl = u_["gc"][c - 1:c, :]
        s_scr[bi, hh] = (u_["s_old"] * jnp.exp(gl)
                         + _bdot_tn(u_["k"] * jnp.exp(gl - u_["gc"]), u_["v_new"]))
        o = u_["o"]
        o = o * lax.rsqrt(jnp.mean(o * o, axis=-1, keepdims=True) + EPS) * nw
        outs[hh][bi] = o * _silu(z_ref[bi * c:(bi + 1) * c, hh * GDN_DK:(hh + 1) * GDN_DK])

    for hh in range(hb):
        col = outs[hh][0] if bb == 1 else jnp.concatenate(outs[hh], axis=0)
        o_ref[:, hh * GDN_DV:(hh + 1) * GDN_DV] = col.astype(o_ref.dtype)

    @pl.when(n == nchunks - 1)
    def _():
        sout_ref[...] = s_scr[...]


def gdn_mixer(proj, small, conv_w, conv_state, head_params, norm_w, s0, layer,
              *, row0, nb, seq, c, bb, hb, o_buf=None, st_buf=None, out_layer=0):
    nchunks = seq // c
    assert bb == 1 or nchunks == 1
    assert c & (c - 1) == 0
    rows = bb * c
    wblk = hb * GDN_DK
    rblk0 = row0 // rows

    def row_map(col0):
        return lambda b, g, n: (rblk0 + b * nchunks + n, col0 // wblk + g)

    def cw_map(col0):
        return lambda b, g, n: (0, col0 // wblk + g)

    def cs_map(col0):
        return lambda b, g, n: (layer, b, 0, col0 // wblk + g)

    qkv_cols = (COL_QKV_A, COL_QKV_A + GDN_QK, COL_QKV_A + 2 * GDN_QK)
    in_specs = (
        [pl.BlockSpec((rows, wblk), row_map(c0)) for c0 in qkv_cols]
        + [pl.BlockSpec((rows, wblk), row_map(COL_Z_A)),
           pl.BlockSpec((rows, N_SMALL), lambda b, g, n: (rblk0 + b * nchunks + n, 0))]
        + [pl.BlockSpec((CONV_K, wblk), cw_map(c0)) for c0 in qkv_cols]
        + [pl.BlockSpec((None, bb, SUBLANES, wblk), cs_map(c0)) for c0 in qkv_cols]
        + [pl.BlockSpec((SUBLANES, LANES), lambda b, g, n: (0, 0)),
           pl.BlockSpec((1, GDN_DV), lambda b, g, n: (0, 0)),
           pl.BlockSpec((None, bb, hb, GDN_DK, GDN_DV), lambda b, g, n: (layer, b, g, 0, 0))])
    operands = (proj, proj, proj, proj, small, conv_w, conv_w, conv_w,
                conv_state, conv_state, conv_state, head_params, norm_w, s0)
    scratch = ([pltpu.VMEM((bb, hb, GDN_DK, GDN_DV), F32)]
               + [pltpu.VMEM((bb, SUBLANES + c, wblk), F32)] * 3
               + [pltpu.VMEM((bb, LANES, 2 * c), F32)])
    return _mixer_call(
        functools.partial(_gdn_kernel, c=c, bb=bb, hb=hb, nchunks=nchunks), "gdn_mixer",
        (nb // bb, GDN_HEADS // hb, nchunks), in_specs, operands, scratch,
        branch=0, rows=rows, wv=wblk, rblk0=rblk0, o_buf=o_buf, st_buf=st_buf,
        out_layer=out_layer, m_total=proj.shape[0], nb=nb, heads=GDN_HEADS, dk=GDN_DK, dv=GDN_DV, ns=bb, hb=hb)


def _ret_kernel(q_ref, k_ref, v_ref, gt_ref, tab_ref, lg_ref, s0_ref,
                o_ref, sout_ref, s_scr, *, c, bb, hb, chained):
    n = pl.program_id(2)

    @pl.when(n == 0)
    def _():
        s_scr[...] = s0_ref[...]

    ii, jj = _iota2((c, c))
    incl = ii >= jj
    dpos = (ii - jj).astype(F32)
    trow = lax.broadcasted_iota(jnp.int32, (c, RET_DK), 0).astype(F32)

    def rope(x, bi):
        rows = slice(bi * c, (bi + 1) * c) if chained else slice(0, c)
        return x * tab_ref[0, rows, :] + pltpu.roll(x, RET_DK // 2, axis=1) * tab_ref[1, rows, :]

    units = [(bi, hh) for bi in range(bb) for hh in range(hb)]
    hd = {}
    for hh in range(hb):
        lg = lg_ref[hh, 0:1, :]
        lg_k = lg[:, 0:RET_DK]
        hd[hh] = dict(decay=jnp.where(incl, jnp.exp(dpos * lg[:, 0:c]), 0.0),
                      q_scale=jnp.exp((trow + 1.0) * lg_k),
                      k_scale=jnp.exp((c - 1.0 - trow) * lg_k),
                      chunk_decay=jnp.exp(c * lg))
    st = {}
    for bi, hh in units:
        rows = slice(bi * c, (bi + 1) * c)
        q = rope(q_ref[rows, hh * RET_DK:(hh + 1) * RET_DK], bi)
        k = rope(k_ref[rows, hh * RET_DK:(hh + 1) * RET_DK], bi) * (RET_DK ** -0.5)
        st[bi, hh] = dict(q=q, k=k, scores=_bdot_nt(q, k) * hd[hh]["decay"])
    for bi, hh in units:
        u_ = st[bi, hh]
        v = v_ref[bi * c:(bi + 1) * c, hh * RET_DV:(hh + 1) * RET_DV]
        u_["intra"] = _bdot(u_["scores"], v)
        u_["kv"] = _bdot_tn(u_["k"] * hd[hh]["k_scale"], v)

    outs = [[None] * bb for _ in range(hb)]
    for bi in range(bb):
        si = 0 if chained else bi
        for hh in range(hb):
            u_ = st[bi, hh]
            u_["s_old"] = s_scr[si, hh]
            u_["inter"] = _bdot(u_["q"] * hd[hh]["q_scale"], u_["s_old"])
        for hh in range(hb):
            u_ = st[bi, hh]
            s_scr[si, hh] = u_["s_old"] * hd[hh]["chunk_decay"] + u_["kv"]
            o = u_["inter"] + u_["intra"]
            mu = jnp.mean(o, axis=-1, keepdims=True)
            var = jnp.mean(jnp.square(o - mu), axis=-1, keepdims=True)
            on = (o - mu) * lax.rsqrt(var + 1e-5)
            outs[hh][bi] = on * _silu(gt_ref[bi * c:(bi + 1) * c, hh * RET_DV:(hh + 1) * RET_DV])

    for hh in range(hb):
        col = outs[hh][0] if bb == 1 else jnp.concatenate(outs[hh], axis=0)
        o_ref[:, hh * RET_DV:(hh + 1) * RET_DV] = col.astype(o_ref.dtype)

    @pl.when(n == pl.num_programs(2) - 1)
    def _():
        sout_ref[...] = s_scr[...]


def ret_mixer(proj, rope_tab, lg_tab, s0, layer, *, row0, nb, seq, c, bb, hb,
              o_buf=None, st_buf=None, out_layer=0):
    grid, chained, ns, rows, rblk0, row_map = _mixer_grid(row0, nb, seq, c, bb, RET_HEADS // hb)
    wk = hb * RET_DK
    wv = hb * RET_DV
    tab_rows = rows if chained else c
    in_specs = [pl.BlockSpec((rows, wk), row_map(COL_Q_B, wk)),
                pl.BlockSpec((rows, wk), row_map(COL_K_B, wk)),
                pl.BlockSpec((rows, wv), row_map(COL_V_B, wv)),
                pl.BlockSpec((rows, wv), row_map(COL_G_B, wv)),
                pl.BlockSpec((2, tab_rows, RET_DK), lambda b, g, n: (0, n, 0)),
                pl.BlockSpec((hb, SUBLANES, RET_DV), lambda b, g, n: (g, 0, 0)),
                pl.BlockSpec((None, ns, hb, RET_DK, RET_DV), lambda b, g, n: (layer, b, g, 0, 0))]
    operands = (proj, proj, proj, proj, rope_tab, lg_tab, s0)
    scratch = [pltpu.VMEM((ns, hb, RET_DK, RET_DV), F32)]
    return _mixer_call(
        functools.partial(_ret_kernel, c=c, bb=bb, hb=hb, chained=chained), "ret_mixer",
        grid, in_specs, operands, scratch,
        branch=1, rows=rows, wv=wv, rblk0=rblk0, o_buf=o_buf, st_buf=st_buf,
        out_layer=out_layer, m_total=proj.shape[0], nb=nb, heads=RET_HEADS, dk=RET_DK, dv=RET_DV, ns=ns, hb=hb)


def _gla_kernel(q_ref, k_ref, v_ref, r_ref, sm_ref, wa_ref, ba_ref, nw_ref, s0_ref,
                o_ref, sout_ref, s_scr, *, c, bb, hb, chained):
    n = pl.program_id(2)

    @pl.when(n == 0)
    def _():
        s_scr[...] = s0_ref[...]

    ii, jj = _iota2((c, c))
    incl = ii >= jj
    lmat = incl.astype(F32)
    ones = jnp.ones((c, LANES), F32)
    nw = nw_ref[...]

    units = [(bi, hh) for bi in range(bb) for hh in range(hb)]
    shared = {}
    for bi in range(bb):
        rows = slice(bi * c, (bi + 1) * c)
        pre = jnp.dot(_bf(sm_ref[rows, :]), wa_ref[...], preferred_element_type=F32) + ba_ref[...]
        log_a_all = jax.nn.log_sigmoid(pre) / GLA_TAU
        shared[bi] = dict(
            bcum=_dot01(lmat, log_a_all),
            a_col=jnp.exp(_dot01_tn(log_a_all, ones)))
    st = {}
    for bi, hh in units:
        rows = slice(bi * c, (bi + 1) * c)
        kc = slice(hh * GLA_DK, (hh + 1) * GLA_DK)
        bcum = shared[bi]["bcum"][:, kc]
        b_last = bcum[c - 1:c, :]
        k = k_ref[rows, kc]
        qd = q_ref[rows, kc] * (GLA_DK ** -0.5) * jnp.exp(bcum)
        kd = k * jnp.exp(-bcum)
        st[bi, hh] = dict(qd=qd, kl=k * jnp.exp(b_last - bcum),
                          attn=jnp.where(incl, _bdot_nt(qd, kd), 0.0))
    for bi, hh in units:
        v = v_ref[bi * c:(bi + 1) * c, hh * GLA_DV:(hh + 1) * GLA_DV]
        st[bi, hh]["intra"] = _bdot(st[bi, hh]["attn"], v)

    outs = [[None] * bb for _ in range(hb)]
    for bi in range(bb):
        si = 0 if chained else bi
        rows = slice(bi * c, (bi + 1) * c)
        for hh in range(hb):
            u_ = st[bi, hh]
            u_["kv"] = _bdot_tn(u_["kl"], v_ref[rows, hh * GLA_DV:(hh + 1) * GLA_DV])
            u_["s_old"] = s_scr[si, hh]
            u_["inter"] = _bdot(u_["qd"], u_["s_old"])
        for hh in range(hb):
            u_ = st[bi, hh]
            a_col = shared[bi]["a_col"][hh * GLA_DK:(hh + 1) * GLA_DK, :]
            a_full = jnp.concatenate([a_col] * (GLA_DV // LANES), axis=1)
            s_scr[si, hh] = u_["s_old"] * a_full + u_["kv"]
            o = u_["inter"] + u_["intra"]
            on = o * lax.rsqrt(jnp.mean(o * o, axis=-1, keepdims=True) + EPS) * nw
            outs[hh][bi] = on * _silu(r_ref[rows, hh * GLA_DV:(hh + 1) * GLA_DV])

    for hh in range(hb):
        col = outs[hh][0] if bb == 1 else jnp.concatenate(outs[hh], axis=0)
        o_ref[:, hh * GLA_DV:(hh + 1) * GLA_DV] = col.astype(o_ref.dtype)

    @pl.when(n == pl.num_programs(2) - 1)
    def _():
        sout_ref[...] = s_scr[...]


def gla_mixer(proj, small, wa_pad, b_a, norm_w, s0, layer, *, row0, nb, seq, c, bb, hb,
              o_buf=None, st_buf=None, out_layer=0):
    grid, chained, ns, rows, rblk0, row_map = _mixer_grid(row0, nb, seq, c, bb, GLA_HEADS // hb)
    wk = hb * GLA_DK
    wv = hb * GLA_DV
    in_specs = [pl.BlockSpec((rows, wk), row_map(COL_Q_C, wk)),
                pl.BlockSpec((rows, wk), row_map(COL_K_C, wk)),
                pl.BlockSpec((rows, wv), row_map(COL_V_C, wv)),
                pl.BlockSpec((rows, wv), row_map(COL_R_C, wv)),
                pl.BlockSpec((rows, N_SMALL), lambda b, g, n: (rblk0 + b * grid[2] + n, 0)),
                pl.BlockSpec((N_SMALL, wk), lambda b, g, n: (0, g)),
                pl.BlockSpec((1, wk), lambda b, g, n: (0, g)),
                pl.BlockSpec((1, GLA_DV), lambda b, g, n: (0, 0)),
                pl.BlockSpec((None, ns, hb, GLA_DK, GLA_DV), lambda b, g, n: (layer, b, g, 0, 0))]
    operands = (proj, proj, proj, proj, small, wa_pad, b_a, norm_w, s0)
    scratch = [pltpu.VMEM((ns, hb, GLA_DK, GLA_DV), F32)]
    return _mixer_call(
        functools.partial(_gla_kernel, c=c, bb=bb, hb=hb, chained=chained), "gla_mixer",
        grid, in_specs, operands, scratch,
        branch=2, rows=rows, wv=wv, rblk0=rblk0, o_buf=o_buf, st_buf=st_buf,
        out_layer=out_layer, m_total=proj.shape[0], nb=nb, heads=GLA_HEADS, dk=GLA_DK, dv=GLA_DV, ns=ns, hb=hb)


def _rope_table(pos):
    half = RET_DK // 2
    inv = ROPE_BASE ** (-jnp.arange(half, dtype=F32) / half)
    ang = pos.astype(F32)[:, None] * inv[None, :]
    cos, sin = jnp.cos(ang), jnp.sin(ang)
    return jnp.stack([jnp.concatenate([cos, cos], axis=-1),
                      jnp.concatenate([-sin, sin], axis=-1)], axis=0)


def _reorder_w_in(w):
    a0 = CONV_DIM + D_MIX
    b0 = a0 + 2 * GDN_HEADS
    lr0 = b0 + 2 * RET_QK + 2 * D_MIX + 2 * GLA_QK + 2 * D_MIX
    g0 = lr0 + GLA_RANK
    main = jnp.concatenate([_bf(w[:, :a0]), _bf(w[:, b0:lr0]), _bf(w[:, g0:])], axis=1)
    small = jnp.concatenate(
        [_bf(w[:, a0:b0]), _bf(w[:, lr0:g0]),
         jnp.zeros((w.shape[0], N_SMALL - 2 * GDN_HEADS - GLA_RANK), BF16)], axis=1)
    return main, small


def _ffn(x, norm_w, wg, wu, wd):
    xn = rmsnorm(x, norm_w, BF16)
    h = swiglu_up(xn, wg, wu)
    return matmul_residual(h, wd, x, 0.5, TK_DOWN, "ffn_down")


def kernel(x_prompt, x_sample, state_gdn, state_gdn_conv, state_ret, state_gla, norm_ffn1, ffn1_gate, ffn1_up, ffn1_down, norm_mix, w_in, gdn_conv_w, gdn_a_log, gdn_dt_bias, gdn_norm_w, gla_w_a2, gla_b_a, gla_norm_w, w_branch, w_out, norm_ffn2, ffn2_gate, ffn2_up, ffn2_down, norm_final):
    bp, lp, _ = x_prompt.shape
    bs, ls, _ = x_sample.shape
    mp = bp * lp
    ms = bs * ls
    x = jnp.concatenate([x_prompt.reshape(mp, D_MODEL), x_sample.reshape(ms, D_MODEL)], axis=0)

    rope_p = _rope_table(jnp.arange(lp, dtype=jnp.int32))
    rope_s = _rope_table(PAST_LEN + jnp.arange(ls, dtype=jnp.int32))
    log_gamma = jnp.log1p(-jnp.exp2(-5.0 - jnp.arange(RET_HEADS, dtype=F32)))
    lg_tab = jnp.broadcast_to(log_gamma[:, None, None], (RET_HEADS, SUBLANES, RET_DV))

    zero_gdn = jnp.zeros((1, bp, GDN_HEADS, GDN_DK, GDN_DV), F32)
    zero_ret = jnp.zeros((1, bp, RET_HEADS, RET_DK, RET_DV), F32)
    zero_gla = jnp.zeros((1, bp, GLA_HEADS, GLA_DK, GLA_DV), F32)
    zero_conv = jnp.zeros((1, bp, SUBLANES, CONV_DIM), F32)
    conv_s_pad = jnp.pad(state_gdn_conv, ((0, 0), (0, 0), (SUBLANES - (CONV_K - 1), 0), (0, 0)))

    cp = math.gcd(lp, CHUNK)
    cs = math.gcd(ls, CHUNK)
    gcp = math.gcd(lp, GLA_CHUNK)
    gcs = math.gcd(ls, GLA_CHUNK)
    pc, sc = MIXER_CFG_PROMPT, MIXER_CFG_SAMPLE

    gdn_p = jnp.zeros((DEPTH, bp, GDN_HEADS, GDN_DK, GDN_DV), F32)
    ret_p = jnp.zeros((DEPTH, bp, RET_HEADS, RET_DK, RET_DV), F32)
    gla_p = jnp.zeros((DEPTH, bp, GLA_HEADS, GLA_DK, GLA_DV), F32)
    gdn_s = jnp.zeros((DEPTH, bs, GDN_HEADS, GDN_DK, GDN_DV), F32)
    ret_s = jnp.zeros((DEPTH, bs, RET_HEADS, RET_DK, RET_DV), F32)
    gla_s = jnp.zeros((DEPTH, bs, GLA_HEADS, GLA_DK, GLA_DV), F32)
    conv_p, conv_s = [], []
    for i in range(DEPTH):
        x = _ffn(x, norm_ffn1[i], ffn1_gate[i], ffn1_up[i], ffn1_down[i])

        xn = rmsnorm(x, norm_mix[i], BF16)
        w_main, w_small = _reorder_w_in(w_in[i])
        proj = matmul(xn, w_main, F32, TN_PROJ, "in_proj")
        small = matmul(xn, w_small, F32, N_SMALL, "in_proj_small")

        head_params = jnp.zeros((SUBLANES, LANES), F32)
        head_params = head_params.at[0, :GDN_HEADS].set(gdn_a_log[i])
        head_params = head_params.at[1, :GDN_HEADS].set(gdn_dt_bias[i])
        gnw = gdn_norm_w[i].reshape(1, GDN_DV)
        wa_pad = jnp.zeros((N_SMALL, GLA_QK), F32).at[LANE_LR:LANE_LR + GLA_RANK].set(gla_w_a2[i])
        wa_pad = _bf(wa_pad)
        b_a = gla_b_a[i].reshape(1, GLA_QK)
        lnw = gla_norm_w[i].reshape(1, GLA_DV)

        o_all, gdn_p = gdn_mixer(proj, small, gdn_conv_w[i], zero_conv, head_params, gnw, zero_gdn, 0,
                                 row0=0, nb=bp, seq=lp, c=cp, **pc["gdn"],
                                 o_buf=jnp.zeros((N_BRANCH, mp + ms, D_MIX), BF16), st_buf=gdn_p, out_layer=i)
        o_all, gdn_s = gdn_mixer(proj, small, gdn_conv_w[i], conv_s_pad, head_params, gnw, state_gdn, i,
                                 row0=mp, nb=bs, seq=ls, c=cs, **sc["gdn"],
                                 o_buf=o_all, st_buf=gdn_s, out_layer=i)
        o_all, ret_p = ret_mixer(proj, rope_p, lg_tab, zero_ret, 0,
                                 row0=0, nb=bp, seq=lp, c=cp, **pc["ret"],
                                 o_buf=o_all, st_buf=ret_p, out_layer=i)
        o_all, ret_s = ret_mixer(proj, rope_s, lg_tab, state_ret, i,
                                 row0=mp, nb=bs, seq=ls, c=cs, **sc["ret"],
                                 o_buf=o_all, st_buf=ret_s, out_layer=i)
        o_all, gla_p = gla_mixer(proj, small, wa_pad, b_a, lnw, zero_gla, 0,
                                 row0=0, nb=bp, seq=lp, c=gcp, **pc["gla"],
                                 o_buf=o_all, st_buf=gla_p, out_layer=i)
        o_all, gla_s = gla_mixer(proj, small, wa_pad, b_a, lnw, state_gla, i,
                                 row0=mp, nb=bs, seq=ls, c=gcs, **sc["gla"],
                                 o_buf=o_all, st_buf=gla_s, out_layer=i)

        tail_p = proj[:mp].reshape(bp, lp, N_MAIN)[:, lp - (CONV_K - 1):, :CONV_DIM]
        tail_s = proj[mp:].reshape(bs, ls, N_MAIN)[:, max(ls - (CONV_K - 1), 0):, :CONV_DIM]
        conv_p.append(tail_p)
        conv_s.append(jnp.concatenate([state_gdn_conv[i], tail_s], axis=1)[:, -(CONV_K - 1):, :])

        merged = merge_branches(o_all, w_branch[i], proj)
        x = matmul_residual(merged, w_out[i], x, 1.0, D_MODEL, "out_proj")

        x = _ffn(x, norm_ffn2[i], ffn2_gate[i], ffn2_up[i], ffn2_down[i])

    y = rmsnorm(x, norm_final, F32)
    y_prompt = y[:mp].reshape(bp, lp, D_MODEL)
    y_sample = y[mp:].reshape(bs, ls, D_MODEL)
    return (y_prompt, y_sample, gdn_p, jnp.stack(conv_p, axis=0), ret_p, gla_p,
            gdn_s, jnp.stack(conv_s, axis=0), ret_s, gla_s)
```

```python
import functools
import math

import jax
import jax.numpy as jnp
from jax import lax
from jax.experimental import pallas as pl
from jax.experimental.pallas import tpu as pltpu

F32 = jnp.float32
BF16 = jnp.bfloat16

D_MODEL = 4096
DEPTH = 2
PAST_LEN = 16384
N_BRANCH = 3
D_MIX = D_MODEL // 2
GDN_DK = 128
GDN_HEADS = D_MIX // 128
GDN_DV = D_MIX // GDN_HEADS
GDN_QK = GDN_HEADS * GDN_DK
CONV_K = 4
CONV_DIM = 2 * GDN_QK + D_MIX
RET_DV = 256
RET_HEADS = D_MIX // RET_DV
RET_DK = RET_DV // 2
RET_QK = RET_HEADS * RET_DK
ROPE_BASE = 10000.0
GLA_HEADS = 4
GLA_DV = D_MIX // GLA_HEADS
GLA_DK = GLA_DV // 2
GLA_QK = GLA_HEADS * GLA_DK
GLA_RANK = 16
GLA_TAU = 16.0
CHUNK = 64
GLA_CHUNK = 16
D_FF = 11008
EPS = 1e-6

COL_QKV_A = 0
COL_Z_A = 6144
COL_Q_B = 8192
COL_K_B = 9216
COL_V_B = 10240
COL_G_B = 12288
COL_Q_C = 14336
COL_K_C = 15360
COL_V_C = 16384
COL_R_C = 18432
COL_GATE = 20480
N_MAIN = 32768
LANE_A = 0
LANE_B = 16
LANE_LR = 32
N_SMALL = 128

LANES = 128
SUBLANES = 8
MXU_DIM = 256
VMEM_LIMIT = 56 * 1024 * 1024

TM = 1024
TN_PROJ = 1024
TN_UP = MXU_DIM
TM_DOWN = 2304
TN_DOWN = 1024
TK_DOWN = 512
TN_OUT = 512
TM_NORM = 256
MIXER_CFG_PROMPT = {"gdn": dict(bb=1, hb=8), "ret": dict(bb=2, hb=8), "gla": dict(bb=4, hb=4)}
MIXER_CFG_SAMPLE = {"gdn": dict(bb=4, hb=4), "ret": dict(bb=4, hb=4), "gla": dict(bb=2, hb=4)}


def _cparams(sem):
    return pltpu.CompilerParams(dimension_semantics=sem, vmem_limit_bytes=VMEM_LIMIT)


def _bf(x):
    return x.astype(BF16)


def _bdot(a, b):
    return jnp.dot(_bf(a), _bf(b), preferred_element_type=F32)


def _bdot_nt(a, b):
    return lax.dot_general(_bf(a), _bf(b), (((1,), (1,)), ((), ())), preferred_element_type=F32)


def _bdot_tn(a, b):
    return lax.dot_general(_bf(a), _bf(b), (((0,), (0,)), ((), ())), preferred_element_type=F32)


def _split3(x):
    hi = _bf(x)
    r1 = x - hi.astype(F32)
    mid = _bf(r1)
    lo = _bf(r1 - mid.astype(F32))
    return hi, mid, lo


def _dot01(m01, x):
    m = _bf(m01)
    hi, mid, lo = _split3(x)
    acc = jnp.dot(m, lo, preferred_element_type=F32)
    acc = acc + jnp.dot(m, mid, preferred_element_type=F32)
    return acc + jnp.dot(m, hi, preferred_element_type=F32)


def _dot01_tn(x, m01):
    m = _bf(m01)
    dn = (((0,), (0,)), ((), ()))
    hi, mid, lo = _split3(x)
    acc = lax.dot_general(lo, m, dn, preferred_element_type=F32)
    acc = acc + lax.dot_general(mid, m, dn, preferred_element_type=F32)
    return acc + lax.dot_general(hi, m, dn, preferred_element_type=F32)


def _silu(x):
    return x * jax.nn.sigmoid(x)


def _iota2(shape):
    return (lax.broadcasted_iota(jnp.int32, shape, 0), lax.broadcasted_iota(jnp.int32, shape, 1))


def _rms_kernel(x_ref, w_ref, o_ref):
    x = x_ref[...]
    y = x * lax.rsqrt(jnp.mean(x * x, axis=-1, keepdims=True) + EPS)
    o_ref[...] = (y * w_ref[...]).astype(o_ref.dtype)


def rmsnorm(x, w, out_dtype, row0=0, nrows=None):
    d = x.shape[1]
    nrows = x.shape[0] - row0 if nrows is None else nrows
    blk0 = row0 // TM_NORM
    return pl.pallas_call(
        _rms_kernel,
        grid=(nrows // TM_NORM,),
        in_specs=[pl.BlockSpec((TM_NORM, d), lambda i: (blk0 + i, 0)),
                  pl.BlockSpec((1, d), lambda i: (0, 0))],
        out_specs=pl.BlockSpec((TM_NORM, d), lambda i: (i, 0)),
        out_shape=jax.ShapeDtypeStruct((nrows, d), out_dtype),
        compiler_params=_cparams(("parallel",)),
        name="rmsnorm",
    )(x, w.reshape(1, d))


def _mm_kernel(x_ref, w_ref, o_ref):
    o_ref[...] = jnp.dot(x_ref[...], w_ref[...], preferred_element_type=F32).astype(o_ref.dtype)


def matmul(x, w, out_dtype, tn, name):
    m, k = x.shape
    n = w.shape[1]
    return pl.pallas_call(
        _mm_kernel,
        grid=(m // TM, n // tn),
        in_specs=[pl.BlockSpec((TM, k), lambda i, j: (i, 0)),
                  pl.BlockSpec((k, tn), lambda i, j: (0, j))],
        out_specs=pl.BlockSpec((TM, tn), lambda i, j: (i, j)),
        out_shape=jax.ShapeDtypeStruct((m, n), out_dtype),
        compiler_params=_cparams(("parallel", "parallel")),
        name=name,
    )(x, w)


def _swiglu_kernel(x_ref, wg_ref, wu_ref, o_ref, wg_bf, wu_bf):
    @pl.when(pl.program_id(1) == 0)
    def _():
        wg_bf[...] = _bf(wg_ref[...])
        wu_bf[...] = _bf(wu_ref[...])

    x = x_ref[...]
    g = jnp.dot(x, wg_bf[...], preferred_element_type=F32)
    u = jnp.dot(x, wu_bf[...], preferred_element_type=F32)
    o_ref[...] = (_silu(g) * u).astype(o_ref.dtype)


def swiglu_up(xn, wg, wu, layer):
    m, k = xn.shape
    n = wg.shape[2]
    return pl.pallas_call(
        _swiglu_kernel,
        grid=(n // TN_UP, m // TM),
        in_specs=[pl.BlockSpec((TM, k), lambda j, i: (i, 0)),
                  pl.BlockSpec((None, k, TN_UP), lambda j, i: (layer, 0, j)),
                  pl.BlockSpec((None, k, TN_UP), lambda j, i: (layer, 0, j))],
        out_specs=pl.BlockSpec((TM, TN_UP), lambda j, i: (i, j)),
        out_shape=jax.ShapeDtypeStruct((m, n), BF16),
        scratch_shapes=[pltpu.VMEM((k, TN_UP), BF16)] * 2,
        compiler_params=_cparams(("arbitrary", "arbitrary")),
        name="swiglu_up",
    )(xn, wg, wu)


def _mm_res_kernel(h_ref, w_ref, x_ref, o_ref, *, scale, nk, k_last):
    kk = pl.program_id(2)
    tk = h_ref.shape[1]

    def part(kv):
        return scale * jnp.dot(h_ref[:, :kv], _bf(w_ref[:kv, :]), preferred_element_type=F32)

    @pl.when(kk == 0)
    def _():
        o_ref[...] = x_ref[...] + part(tk if nk > 1 else k_last)

    if nk > 2:
        @pl.when(jnp.logical_and(kk > 0, kk < nk - 1))
        def _():
            o_ref[...] += part(tk)

    if nk > 1:
        @pl.when(kk == nk - 1)
        def _():
            o_ref[...] += part(k_last)


def matmul_residual(h, w, layer, x, scale, tm, tn, tk, name):
    m, k = h.shape
    n = w.shape[2]
    nk = pl.cdiv(k, tk)
    k_last = k - (nk - 1) * tk
    return pl.pallas_call(
        functools.partial(_mm_res_kernel, scale=scale, nk=nk, k_last=k_last),
        grid=(m // tm, n // tn, nk),
        in_specs=[pl.BlockSpec((tm, tk), lambda i, j, kk: (i, kk)),
                  pl.BlockSpec((None, tk, tn), lambda i, j, kk: (layer, kk, j)),
                  pl.BlockSpec((tm, tn), lambda i, j, kk: (i, j))],
        out_specs=pl.BlockSpec((tm, tn), lambda i, j, kk: (i, j)),
        out_shape=jax.ShapeDtypeStruct((m, n), F32),
        compiler_params=_cparams(("parallel", "parallel", "arbitrary")),
        name=name,
    )(h, w, x)


def _merge_kernel(o_ref, w_ref, g_ref, out_ref, acc_ref):
    r = pl.program_id(2)
    y = jnp.dot(o_ref[...], _bf(w_ref[...]), preferred_element_type=F32)
    gated = jax.nn.sigmoid(g_ref[...]) * y

    @pl.when(r == 0)
    def _():
        acc_ref[...] = gated

    @pl.when(r > 0)
    def _():
        acc_ref[...] += gated

    @pl.when(r == N_BRANCH - 1)
    def _():
        out_ref[...] = acc_ref[...].astype(out_ref.dtype)


def merge_branches(o_all, w_branch, layer, proj):
    _, m, k = o_all.shape
    n = w_branch.shape[3]
    gate_blk0 = COL_GATE // TN_PROJ
    per_branch = D_MODEL // TN_PROJ
    return pl.pallas_call(
        _merge_kernel,
        grid=(m // TM, n // TN_PROJ, N_BRANCH),
        in_specs=[pl.BlockSpec((None, TM, k), lambda i, j, r: (r, i, 0)),
                  pl.BlockSpec((None, None, k, TN_PROJ), lambda i, j, r: (layer, r, 0, j)),
                  pl.BlockSpec((TM, TN_PROJ), lambda i, j, r: (i, gate_blk0 + r * per_branch + j))],
        out_specs=pl.BlockSpec((TM, TN_PROJ), lambda i, j, r: (i, j)),
        out_shape=jax.ShapeDtypeStruct((m, n), BF16),
        scratch_shapes=[pltpu.VMEM((TM, TN_PROJ), F32)],
        compiler_params=_cparams(("parallel", "parallel", "arbitrary")),
        name="merge_branches",
    )(o_all, w_branch, proj)


def _mixer_call(body, name, grid, in_specs, operands, scratch, *, branch, rows, wv, rblk0,
                o_buf, st_buf, out_layer, m_total, nb, heads, dk, dv, ns, hb):
    steps = grid[2]
    out_specs = [pl.BlockSpec((None, rows, wv), lambda b, g, n: (branch, rblk0 + b * steps + n, g)),
                 pl.BlockSpec((None, ns, hb, dk, dv), lambda b, g, n: (out_layer, b, g, 0, 0))]
    out_shape = [jax.ShapeDtypeStruct((N_BRANCH, m_total, D_MIX), BF16),
                 jax.ShapeDtypeStruct((DEPTH, nb, heads, dk, dv), F32)]
    aliases = {}
    operands = list(operands)
    in_specs = list(in_specs)
    n_alias = 0
    for out_idx, buf in ((0, o_buf), (1, st_buf)):
        if buf is not None:
            aliases[len(operands)] = out_idx
            operands.append(buf)
            in_specs.append(pl.BlockSpec(memory_space=pl.ANY))
            n_alias += 1

    def wrapped(*refs):
        n_in = len(operands)
        body(*refs[:n_in - n_alias], *refs[n_in:])

    return pl.pallas_call(
        wrapped,
        grid=grid,
        in_specs=in_specs,
        out_specs=out_specs,
        out_shape=out_shape,
        scratch_shapes=scratch,
        input_output_aliases=aliases,
        compiler_params=_cparams(("parallel", "parallel", "arbitrary")),
        name=name,
    )(*operands)


def _mixer_grid(row0, nb, seq, c, bb, head_groups):
    nchunks = seq // c
    chained = nchunks > 1
    if chained:
        assert nchunks % bb == 0
        grid = (nb, head_groups, nchunks // bb)
    else:
        assert nb % bb == 0
        grid = (nb // bb, head_groups, 1)
    rows = bb * c
    assert row0 % rows == 0
    rblk0 = row0 // rows
    steps = grid[2]

    def row_map(col0, w):
        return lambda b, g, n: (rblk0 + b * steps + n, col0 // w + g)

    return grid, chained, (1 if chained else bb), rows, rblk0, row_map


def _gdn_kernel(q_ref, k_ref, v_ref, z_ref, sm_ref, cwq_ref, cwk_ref, cwv_ref,
                csq_ref, csk_ref, csv_ref, hp_ref, nw_ref, s0_ref,
                o_ref, sout_ref, s_scr, xq, xk, xv, gct_scr, *, c, bb, hb, nchunks):
    g = pl.program_id(1)
    n = pl.program_id(2)
    pad = SUBLANES
    w2 = 2 * c

    @pl.when(n == 0)
    def _():
        s_scr[...] = s0_ref[...]
        xq[:, 0:pad, :] = csq_ref[...]
        xk[:, 0:pad, :] = csk_ref[...]
        xv[:, 0:pad, :] = csv_ref[...]

    ii, jw = _iota2((c, w2))
    jm = jnp.where(jw >= c, jw - c, jw)
    left = jw < c
    incl_w = ii >= jm
    strict_w = ii > jm
    eye_right = (jw == ii + c).astype(F32)
    cum_w = (ii <= jm).astype(F32)
    ic, jc = _iota2((c, c))
    lmat = (ic >= jc).astype(F32)
    lane = lax.broadcasted_iota(jnp.int32, (c, LANES), 1)
    a_log = hp_ref[0:1, :]
    dt_bias = hp_ref[1:2, :]
    nw = nw_ref[...]

    def conv(x_scr, bi, cw_ref):
        y = x_scr[bi, pl.ds(pad - 3, c), :] * cw_ref[0:1, :]
        for i in range(1, CONV_K):
            y = y + x_scr[bi, pl.ds(pad - 3 + i, c), :] * cw_ref[i:i + 1, :]
        return _silu(y)

    units = [(bi, hh) for bi in range(bb) for hh in range(hb)]
    st = {}
    for bi in range(bb):
        rows = slice(bi * c, (bi + 1) * c)
        xq[bi, pad:pad + c, :] = q_ref[rows, :]
        xk[bi, pad:pad + c, :] = k_ref[rows, :]
        xv[bi, pad:pad + c, :] = v_ref[rows, :]
        qa = conv(xq, bi, cwq_ref)
        ka = conv(xk, bi, cwk_ref)
        va = conv(xv, bi, cwv_ref)
        xq[bi, 0:pad, :] = xq[bi, c:c + pad, :]
        xk[bi, 0:pad, :] = xk[bi, c:c + pad, :]
        xv[bi, 0:pad, :] = xv[bi, c:c + pad, :]

        sm = sm_ref[rows, :]
        g_all = -jnp.exp(a_log) * jax.nn.softplus(sm + dt_bias)
        beta_all = jax.nn.sigmoid(sm)
        gc_all = _dot01(lmat, g_all)
        gct_scr[bi] = _dot01_tn(g_all, cum_w)
        for hh in range(hb):
            h = g * hb + hh
            cols = slice(hh * GDN_DK, (hh + 1) * GDN_DK)
            q = qa[:, cols]
            k = ka[:, cols]
            q = q * lax.rsqrt(jnp.sum(q * q, axis=-1, keepdims=True) + EPS) * (GDN_DK ** -0.5)
            k = k * lax.rsqrt(jnp.sum(k * k, axis=-1, keepdims=True) + EPS)
            gc_col = jnp.sum(jnp.where(lane == h + LANE_A, gc_all, 0.0), axis=-1, keepdims=True)
            beta = jnp.sum(jnp.where(lane == h + LANE_B, beta_all, 0.0), axis=-1, keepdims=True)
            gc = jnp.broadcast_to(gc_col, (c, GDN_DK))
            st[bi, hh] = dict(q=q, k=k, v=va[:, cols], beta=beta, gc_col=gc_col, gc=gc, kb=k * beta)

    for bi, hh in units:
        u_ = st[bi, hh]
        gc_row = gct_scr[bi, pl.ds(g * hb + hh, 1), :]
        decay_w = jnp.where(incl_w, jnp.exp(u_["gc_col"] - gc_row), 0.0)
        r = _bdot_nt(jnp.concatenate([u_["kb"], u_["q"]], axis=0),
                     jnp.concatenate([u_["k"], u_["k"]], axis=0))
        top = r[0:c, :] * decay_w
        u_["qk"] = r[c:w2, 0:c] * decay_w[:, 0:c]
        u_["wmat"] = jnp.where(left, jnp.where(strict_w, -top, 0.0), eye_right)

    for _ in range(c.bit_length() - 1):
        for unit in units:
            wmat = st[unit]["wmat"]
            st[unit]["wmat"] = _bdot(wmat[:, 0:c], wmat) + jnp.where(left, 0.0, wmat)

    for unit in units:
        u_ = st[unit]
        eg = jnp.exp(u_["gc"])
        u_["qe"] = u_["q"] * eg
        u_["uw"] = _bdot(u_["wmat"][:, c:w2],
                         jnp.concatenate([u_["v"] * u_["beta"], u_["kb"] * eg], axis=1))

    for bi, hh in units:
        u_ = st[bi, hh]
        u_["s_old"] = s_scr[bi, hh]
        u_["both"] = _bdot(jnp.concatenate([u_["uw"][:, GDN_DV:], u_["qe"]], axis=0), u_["s_old"])

    for unit in units:
        u_ = st[unit]
        u_["v_new"] = u_["uw"][:, 0:GDN_DV] - u_["both"][0:c, :]
        u_["o"] = u_["both"][c:w2, :] + _bdot(u_["qk"], u_["v_new"])

    outs = [[None] * bb for _ in range(hb)]
    for bi, hh in units:
        u_ = st[bi, hh]
        gl = u_["gc"][c - 1:c, :]
        s_scr[bi, hh] = (u_["s_old"] * jnp.exp(gl)
                         + _bdot_tn(u_["k"] * jnp.exp(gl - u_["gc"]), u_["v_new"]))
        o = u_["o"]
        o = o * lax.rsqrt(jnp.mean(o * o, axis=-1, keepdims=True) + EPS) * nw
        outs[hh][bi] = o * _silu(z_ref[bi * c:(bi + 1) * c, hh * GDN_DK:(hh + 1) * GDN_DK])

    for hh in range(hb):
        col = outs[hh][0] if bb == 1 else jnp.concatenate(outs[hh], axis=0)
        o_ref[:, hh * GDN_DV:(hh + 1) * GDN_DV] = col.astype(o_ref.dtype)

    @pl.when(n == nchunks - 1)
    def _():
        sout_ref[...] = s_scr[...]


def gdn_mixer(proj, small, conv_w, conv_state, head_params, norm_w, s0, layer,
              *, row0, nb, seq, c, bb, hb, o_buf=None, st_buf=None, out_layer=0):
    nchunks = seq // c
    assert bb == 1 or nchunks == 1
    assert c & (c - 1) == 0
    rows = bb * c
    wblk = hb * GDN_DK
    rblk0 = row0 // rows

    def row_map(col0):
        return lambda b, g, n: (rblk0 + b * nchunks + n, col0 // wblk + g)

    def cw_map(col0):
        return lambda b, g, n: (0, col0 // wblk + g)

    def cs_map(col0):
        return lambda b, g, n: (layer, b, 0, col0 // wblk + g)

    qkv_cols = (COL_QKV_A, COL_QKV_A + GDN_QK, COL_QKV_A + 2 * GDN_QK)
    in_specs = (
        [pl.BlockSpec((rows, wblk), row_map(c0)) for c0 in qkv_cols]
        + [pl.BlockSpec((rows, wblk), row_map(COL_Z_A)),
           pl.BlockSpec((rows, N_SMALL), lambda b, g, n: (rblk0 + b * nchunks + n, 0))]
        + [pl.BlockSpec((CONV_K, wblk), cw_map(c0)) for c0 in qkv_cols]
        + [pl.BlockSpec((None, bb, SUBLANES, wblk), cs_map(c0)) for c0 in qkv_cols]
        + [pl.BlockSpec((SUBLANES, LANES), lambda b, g, n: (0, 0)),
           pl.BlockSpec((1, GDN_DV), lambda b, g, n: (0, 0)),
           pl.BlockSpec((None, bb, hb, GDN_DK, GDN_DV), lambda b, g, n: (layer, b, g, 0, 0))])
    operands = (proj, proj, proj, proj, small, conv_w, conv_w, conv_w,
                conv_state, conv_state, conv_state, head_params, norm_w, s0)
    scratch = ([pltpu.VMEM((bb, hb, GDN_DK, GDN_DV), F32)]
               + [pltpu.VMEM((bb, SUBLANES + c, wblk), F32)] * 3
               + [pltpu.VMEM((bb, LANES, 2 * c), F32)])
    return _mixer_call(
        functools.partial(_gdn_kernel, c=c, bb=bb, hb=hb, nchunks=nchunks), "gdn_mixer",
        (nb // bb, GDN_HEADS // hb, nchunks), in_specs, operands, scratch,
        branch=0, rows=rows, wv=wblk, rblk0=rblk0, o_buf=o_buf, st_buf=st_buf,
        out_layer=out_layer, m_total=proj.shape[0], nb=nb, heads=GDN_HEADS, dk=GDN_DK, dv=GDN_DV, ns=bb, hb=hb)


def _ret_kernel(q_ref, k_ref, v_ref, gt_ref, tab_ref, lg_ref, s0_ref,
                o_ref, sout_ref, s_scr, *, c, bb, hb, chained):
    n = pl.program_id(2)

    @pl.when(n == 0)
    def _():
        s_scr[...] = s0_ref[...]

    ii, jj = _iota2((c, c))
    incl = ii >= jj
    dpos = (ii - jj).astype(F32)
    trow = lax.broadcasted_iota(jnp.int32, (c, RET_DK), 0).astype(F32)

    def rope(x, bi):
        rows = slice(bi * c, (bi + 1) * c) if chained else slice(0, c)
        return x * tab_ref[0, rows, :] + pltpu.roll(x, RET_DK // 2, axis=1) * tab_ref[1, rows, :]

    units = [(bi, hh) for bi in range(bb) for hh in range(hb)]
    hd = {}
    for hh in range(hb):
        lg = lg_ref[hh, 0:1, :]
        lg_k = lg[:, 0:RET_DK]
        hd[hh] = dict(decay=jnp.where(incl, jnp.exp(dpos * lg[:, 0:c]), 0.0),
                      q_scale=jnp.exp((trow + 1.0) * lg_k),
                      k_scale=jnp.exp((c - 1.0 - trow) * lg_k),
                      chunk_decay=jnp.exp(c * lg))
    st = {}
    for bi, hh in units:
        rows = slice(bi * c, (bi + 1) * c)
        q = rope(q_ref[rows, hh * RET_DK:(hh + 1) * RET_DK], bi)
        k = rope(k_ref[rows, hh * RET_DK:(hh + 1) * RET_DK], bi) * (RET_DK ** -0.5)
        st[bi, hh] = dict(q=q, k=k, scores=_bdot_nt(q, k) * hd[hh]["decay"])
    for bi, hh in units:
        u_ = st[bi, hh]
        v = v_ref[bi * c:(bi + 1) * c, hh * RET_DV:(hh + 1) * RET_DV]
        u_["intra"] = _bdot(u_["scores"], v)
        u_["kv"] = _bdot_tn(u_["k"] * hd[hh]["k_scale"], v)

    outs = [[None] * bb for _ in range(hb)]
    for bi in range(bb):
        si = 0 if chained else bi
        for hh in range(hb):
            u_ = st[bi, hh]
            u_["s_old"] = s_scr[si, hh]
            u_["inter"] = _bdot(u_["q"] * hd[hh]["q_scale"], u_["s_old"])
        for hh in range(hb):
            u_ = st[bi, hh]
            s_scr[si, hh] = u_["s_old"] * hd[hh]["chunk_decay"] + u_["kv"]
            o = u_["inter"] + u_["intra"]
            mu = jnp.mean(o, axis=-1, keepdims=True)
            var = jnp.mean(jnp.square(o - mu), axis=-1, keepdims=True)
            on = (o - mu) * lax.rsqrt(var + 1e-5)
            outs[hh][bi] = on * _silu(gt_ref[bi * c:(bi + 1) * c, hh * RET_DV:(hh + 1) * RET_DV])

    for hh in range(hb):
        col = outs[hh][0] if bb == 1 else jnp.concatenate(outs[hh], axis=0)
        o_ref[:, hh * RET_DV:(hh + 1) * RET_DV] = col.astype(o_ref.dtype)

    @pl.when(n == pl.num_programs(2) - 1)
    def _():
        sout_ref[...] = s_scr[...]


def ret_mixer(proj, rope_tab, lg_tab, s0, layer, *, row0, nb, seq, c, bb, hb,
              o_buf=None, st_buf=None, out_layer=0):
    grid, chained, ns, rows, rblk0, row_map = _mixer_grid(row0, nb, seq, c, bb, RET_HEADS // hb)
    wk = hb * RET_DK
    wv = hb * RET_DV
    tab_rows = rows if chained else c
    in_specs = [pl.BlockSpec((rows, wk), row_map(COL_Q_B, wk)),
                pl.BlockSpec((rows, wk), row_map(COL_K_B, wk)),
                pl.BlockSpec((rows, wv), row_map(COL_V_B, wv)),
                pl.BlockSpec((rows, wv), row_map(COL_G_B, wv)),
                pl.BlockSpec((2, tab_rows, RET_DK), lambda b, g, n: (0, n, 0)),
                pl.BlockSpec((hb, SUBLANES, RET_DV), lambda b, g, n: (g, 0, 0)),
                pl.BlockSpec((None, ns, hb, RET_DK, RET_DV), lambda b, g, n: (layer, b, g, 0, 0))]
    operands = (proj, proj, proj, proj, rope_tab, lg_tab, s0)
    scratch = [pltpu.VMEM((ns, hb, RET_DK, RET_DV), F32)]
    return _mixer_call(
        functools.partial(_ret_kernel, c=c, bb=bb, hb=hb, chained=chained), "ret_mixer",
        grid, in_specs, operands, scratch,
        branch=1, rows=rows, wv=wv, rblk0=rblk0, o_buf=o_buf, st_buf=st_buf,
        out_layer=out_layer, m_total=proj.shape[0], nb=nb, heads=RET_HEADS, dk=RET_DK, dv=RET_DV, ns=ns, hb=hb)


def _gla_kernel(q_ref, k_ref, v_ref, r_ref, sm_ref, wa_ref, ba_ref, nw_ref, s0_ref,
                o_ref, sout_ref, s_scr, *, c, bb, hb, chained):
    n = pl.program_id(2)

    @pl.when(n == 0)
    def _():
        s_scr[...] = s0_ref[...]

    ii, jj = _iota2((c, c))
    incl = ii >= jj
    lmat = incl.astype(F32)
    ones = jnp.ones((c, LANES), F32)
    nw = nw_ref[...]

    units = [(bi, hh) for bi in range(bb) for hh in range(hb)]
    shared = {}
    for bi in range(bb):
        rows = slice(bi * c, (bi + 1) * c)
        pre = jnp.dot(_bf(sm_ref[rows, :]), wa_ref[...], preferred_element_type=F32) + ba_ref[...]
        log_a_all = jax.nn.log_sigmoid(pre) / GLA_TAU
        shared[bi] = dict(
            bcum=_dot01(lmat, log_a_all),
            a_col=jnp.exp(_dot01_tn(log_a_all, ones)))
    st = {}
    for bi, hh in units:
        rows = slice(bi * c, (bi + 1) * c)
        kc = slice(hh * GLA_DK, (hh + 1) * GLA_DK)
        bcum = shared[bi]["bcum"][:, kc]
        b_last = bcum[c - 1:c, :]
        k = k_ref[rows, kc]
        qd = q_ref[rows, kc] * (GLA_DK ** -0.5) * jnp.exp(bcum)
        kd = k * jnp.exp(-bcum)
        st[bi, hh] = dict(qd=qd, kl=k * jnp.exp(b_last - bcum),
                          attn=jnp.where(incl, _bdot_nt(qd, kd), 0.0))
    for bi, hh in units:
        v = v_ref[bi * c:(bi + 1) * c, hh * GLA_DV:(hh + 1) * GLA_DV]
        st[bi, hh]["intra"] = _bdot(st[bi, hh]["attn"], v)

    outs = [[None] * bb for _ in range(hb)]
    for bi in range(bb):
        si = 0 if chained else bi
        rows = slice(bi * c, (bi + 1) * c)
        for hh in range(hb):
            u_ = st[bi, hh]
            u_["kv"] = _bdot_tn(u_["kl"], v_ref[rows, hh * GLA_DV:(hh + 1) * GLA_DV])
            u_["s_old"] = s_scr[si, hh]
            u_["inter"] = _bdot(u_["qd"], u_["s_old"])
        for hh in range(hb):
            u_ = st[bi, hh]
            a_col = shared[bi]["a_col"][hh * GLA_DK:(hh + 1) * GLA_DK, :]
            a_full = jnp.concatenate([a_col] * (GLA_DV // LANES), axis=1)
            s_scr[si, hh] = u_["s_old"] * a_full + u_["kv"]
            o = u_["inter"] + u_["intra"]
            on = o * lax.rsqrt(jnp.mean(o * o, axis=-1, keepdims=True) + EPS) * nw
            outs[hh][bi] = on * _silu(r_ref[rows, hh * GLA_DV:(hh + 1) * GLA_DV])

    for hh in range(hb):
        col = outs[hh][0] if bb == 1 else jnp.concatenate(outs[hh], axis=0)
        o_ref[:, hh * GLA_DV:(hh + 1) * GLA_DV] = col.astype(o_ref.dtype)

    @pl.when(n == pl.num_programs(2) - 1)
    def _():
        sout_ref[...] = s_scr[...]


def gla_mixer(proj, small, wa_pad, b_a, norm_w, s0, layer, *, row0, nb, seq, c, bb, hb,
              o_buf=None, st_buf=None, out_layer=0):
    grid, chained, ns, rows, rblk0, row_map = _mixer_grid(row0, nb, seq, c, bb, GLA_HEADS // hb)
    wk = hb * GLA_DK
    wv = hb * GLA_DV
    in_specs = [pl.BlockSpec((rows, wk), row_map(COL_Q_C, wk)),
                pl.BlockSpec((rows, wk), row_map(COL_K_C, wk)),
                pl.BlockSpec((rows, wv), row_map(COL_V_C, wv)),
                pl.BlockSpec((rows, wv), row_map(COL_R_C, wv)),
                pl.BlockSpec((rows, N_SMALL), lambda b, g, n: (rblk0 + b * grid[2] + n, 0)),
                pl.BlockSpec((N_SMALL, wk), lambda b, g, n: (0, g)),
                pl.BlockSpec((1, wk), lambda b, g, n: (0, g)),
                pl.BlockSpec((1, GLA_DV), lambda b, g, n: (0, 0)),
                pl.BlockSpec((None, ns, hb, GLA_DK, GLA_DV), lambda b, g, n: (layer, b, g, 0, 0))]
    operands = (proj, proj, proj, proj, small, wa_pad, b_a, norm_w, s0)
    scratch = [pltpu.VMEM((ns, hb, GLA_DK, GLA_DV), F32)]
    return _mixer_call(
        functools.partial(_gla_kernel, c=c, bb=bb, hb=hb, chained=chained), "gla_mixer",
        grid, in_specs, operands, scratch,
        branch=2, rows=rows, wv=wv, rblk0=rblk0, o_buf=o_buf, st_buf=st_buf,
        out_layer=out_layer, m_total=proj.shape[0], nb=nb, heads=GLA_HEADS, dk=GLA_DK, dv=GLA_DV, ns=ns, hb=hb)


def _rope_table(pos):
    half = RET_DK // 2
    inv = ROPE_BASE ** (-jnp.arange(half, dtype=F32) / half)
    ang = pos.astype(F32)[:, None] * inv[None, :]
    cos, sin = jnp.cos(ang), jnp.sin(ang)
    return jnp.stack([jnp.concatenate([cos, cos], axis=-1),
                      jnp.concatenate([-sin, sin], axis=-1)], axis=0)


def _reorder_w_in(w):
    a0 = CONV_DIM + D_MIX
    b0 = a0 + 2 * GDN_HEADS
    lr0 = b0 + 2 * RET_QK + 2 * D_MIX + 2 * GLA_QK + 2 * D_MIX
    g0 = lr0 + GLA_RANK
    main = jnp.concatenate([_bf(w[:, :a0]), _bf(w[:, b0:lr0]), _bf(w[:, g0:])], axis=1)
    small = jnp.concatenate(
        [_bf(w[:, a0:b0]), _bf(w[:, lr0:g0]),
         jnp.zeros((w.shape[0], N_SMALL - 2 * GDN_HEADS - GLA_RANK), BF16)], axis=1)
    return main, small


def _ffn(x, norm_w, wg, wu, wd, layer):
    xn = rmsnorm(x, norm_w, BF16)
    h = swiglu_up(xn, wg, wu, layer)
    return matmul_residual(h, wd, layer, x, 0.5, TM_DOWN, TN_DOWN, TK_DOWN, "ffn_down")


def _tail_rows(row0, nb, seq):
    n_tail = min(seq, CONV_K - 1)
    return [row0 + b * seq + seq - n_tail + t for b in range(nb) for t in range(n_tail)], n_tail


def kernel(x_prompt, x_sample, state_gdn, state_gdn_conv, state_ret, state_gla, norm_ffn1, ffn1_gate, ffn1_up, ffn1_down, norm_mix, w_in, gdn_conv_w, gdn_a_log, gdn_dt_bias, gdn_norm_w, gla_w_a2, gla_b_a, gla_norm_w, w_branch, w_out, norm_ffn2, ffn2_gate, ffn2_up, ffn2_down, norm_final):
    bp, lp, _ = x_prompt.shape
    bs, ls, _ = x_sample.shape
    mp = bp * lp
    ms = bs * ls
    x = jnp.concatenate([x_prompt.reshape(mp, D_MODEL), x_sample.reshape(ms, D_MODEL)], axis=0)

    rope_p = _rope_table(jnp.arange(lp, dtype=jnp.int32))
    rope_s = _rope_table(PAST_LEN + jnp.arange(ls, dtype=jnp.int32))
    log_gamma = jnp.log1p(-jnp.exp2(-5.0 - jnp.arange(RET_HEADS, dtype=F32)))
    lg_tab = jnp.broadcast_to(log_gamma[:, None, None], (RET_HEADS, SUBLANES, RET_DV))

    zero_gdn = jnp.zeros((1, bp, GDN_HEADS, GDN_DK, GDN_DV), F32)
    zero_ret = jnp.zeros((1, bp, RET_HEADS, RET_DK, RET_DV), F32)
    zero_gla = jnp.zeros((1, bp, GLA_HEADS, GLA_DK, GLA_DV), F32)
    zero_conv = jnp.zeros((1, bp, SUBLANES, CONV_DIM), F32)
    conv_s_pad = jnp.pad(state_gdn_conv, ((0, 0), (0, 0), (SUBLANES - (CONV_K - 1), 0), (0, 0)))

    cp = math.gcd(lp, CHUNK)
    cs = math.gcd(ls, CHUNK)
    gcp = math.gcd(lp, GLA_CHUNK)
    gcs = math.gcd(ls, GLA_CHUNK)
    pc, sc = MIXER_CFG_PROMPT, MIXER_CFG_SAMPLE

    gdn_p = jnp.zeros((DEPTH, bp, GDN_HEADS, GDN_DK, GDN_DV), F32)
    ret_p = jnp.zeros((DEPTH, bp, RET_HEADS, RET_DK, RET_DV), F32)
    gla_p = jnp.zeros((DEPTH, bp, GLA_HEADS, GLA_DK, GLA_DV), F32)
    gdn_s = jnp.zeros((DEPTH, bs, GDN_HEADS, GDN_DK, GDN_DV), F32)
    ret_s = jnp.zeros((DEPTH, bs, RET_HEADS, RET_DK, RET_DV), F32)
    gla_s = jnp.zeros((DEPTH, bs, GLA_HEADS, GLA_DK, GLA_DV), F32)
    conv_p, conv_s = [], []
    for i in range(DEPTH):
        x = _ffn(x, norm_ffn1[i], ffn1_gate, ffn1_up, ffn1_down, i)

        xn = rmsnorm(x, norm_mix[i], BF16)
        w_main, w_small = _reorder_w_in(w_in[i])
        proj = matmul(xn, w_main, F32, TN_PROJ, "in_proj")
        small = matmul(xn, w_small, F32, N_SMALL, "in_proj_small")

        head_params = jnp.zeros((SUBLANES, LANES), F32)
        head_params = head_params.at[0, :GDN_HEADS].set(gdn_a_log[i])
        head_params = head_params.at[1, :GDN_HEADS].set(gdn_dt_bias[i])
        gnw = gdn_norm_w[i].reshape(1, GDN_DV)
        wa_pad = jnp.zeros((N_SMALL, GLA_QK), F32).at[LANE_LR:LANE_LR + GLA_RANK].set(gla_w_a2[i])
        wa_pad = _bf(wa_pad)
        b_a = gla_b_a[i].reshape(1, GLA_QK)
        lnw = gla_norm_w[i].reshape(1, GLA_DV)

        o_all, gdn_p = gdn_mixer(proj, small, gdn_conv_w[i], zero_conv, head_params, gnw, zero_gdn, 0,
                                 row0=0, nb=bp, seq=lp, c=cp, **pc["gdn"],
                                 o_buf=jnp.zeros((N_BRANCH, mp + ms, D_MIX), BF16), st_buf=gdn_p, out_layer=i)
        o_all, gdn_s = gdn_mixer(proj, small, gdn_conv_w[i], conv_s_pad, head_params, gnw, state_gdn, i,
                                 row0=mp, nb=bs, seq=ls, c=cs, **sc["gdn"],
                                 o_buf=o_all, st_buf=gdn_s, out_layer=i)
        o_all, ret_p = ret_mixer(proj, rope_p, lg_tab, zero_ret, 0,
                                 row0=0, nb=bp, seq=lp, c=cp, **pc["ret"],
                                 o_buf=o_all, st_buf=ret_p, out_layer=i)
        o_all, ret_s = ret_mixer(proj, rope_s, lg_tab, state_ret, i,
                                 row0=mp, nb=bs, seq=ls, c=cs, **sc["ret"],
                                 o_buf=o_all, st_buf=ret_s, out_layer=i)
        o_all, gla_p = gla_mixer(proj, small, wa_pad, b_a, lnw, zero_gla, 0,
                                 row0=0, nb=bp, seq=lp, c=gcp, **pc["gla"],
                                 o_buf=o_all, st_buf=gla_p, out_layer=i)
        o_all, gla_s = gla_mixer(proj, small, wa_pad, b_a, lnw, state_gla, i,
                                 row0=mp, nb=bs, seq=ls, c=gcs, **sc["gla"],
                                 o_buf=o_all, st_buf=gla_s, out_layer=i)

        rows_p, nt_p = _tail_rows(0, bp, lp)
        rows_s, nt_s = _tail_rows(mp, bs, ls)
        tail_p = jnp.take(proj, jnp.asarray(rows_p, jnp.int32), axis=0)[:, :CONV_DIM].reshape(bp, nt_p, CONV_DIM)
        tail_s = jnp.take(proj, jnp.asarray(rows_s, jnp.int32), axis=0)[:, :CONV_DIM].reshape(bs, nt_s, CONV_DIM)
        conv_p.append(jnp.concatenate([zero_conv[0], tail_p], axis=1)[:, -(CONV_K - 1):, :])
        conv_s.append(jnp.concatenate([state_gdn_conv[i], tail_s], axis=1)[:, -(CONV_K - 1):, :])

        merged = merge_branches(o_all, w_branch, i, proj)
        x = matmul_residual(merged, w_out, i, x, 1.0, TM, TN_OUT, D_MODEL, "out_proj")

        x = _ffn(x, norm_ffn2[i], ffn2_gate, ffn2_up, ffn2_down, i)

    y_prompt = rmsnorm(x, norm_final, F32, 0, mp).reshape(bp, lp, D_MODEL)
    y_sample = rmsnorm(x, norm_final, F32, mp, ms).reshape(bs, ls, D_MODEL)
    return (y_prompt, y_sample, gdn_p, jnp.stack(conv_p, axis=0), ret_p, gla_p,
            gdn_s, jnp.stack(conv_s, axis=0), ret_s, gla_s)
```

```python
import functools
import math

import jax
import jax.numpy as jnp
from jax import lax
from jax.experimental import pallas as pl
from jax.experimental.pallas import tpu as pltpu

F32 = jnp.float32
BF16 = jnp.bfloat16

D_MODEL = 4096
DEPTH = 2
PAST_LEN = 16384
N_BRANCH = 3
D_MIX = D_MODEL // 2
GDN_DK = 128
GDN_HEADS = D_MIX // 128
GDN_DV = D_MIX // GDN_HEADS
GDN_QK = GDN_HEADS * GDN_DK
CONV_K = 4
CONV_DIM = 2 * GDN_QK + D_MIX
RET_DV = 256
RET_HEADS = D_MIX // RET_DV
RET_DK = RET_DV // 2
RET_QK = RET_HEADS * RET_DK
ROPE_BASE = 10000.0
GLA_HEADS = 4
GLA_DV = D_MIX // GLA_HEADS
GLA_DK = GLA_DV // 2
GLA_QK = GLA_HEADS * GLA_DK
GLA_RANK = 16
GLA_TAU = 16.0
CHUNK = 64
GLA_CHUNK = 16
D_FF = 11008
EPS = 1e-6

COL_QKV_A = 0
COL_Z_A = 6144
N_PROJ_A = 8192
COL_Q_B = 0
COL_K_B = 1024
COL_V_B = 2048
COL_G_B = 4096
COL_Q_C = 6144
COL_K_C = 7168
COL_V_C = 8192
COL_R_C = 10240
N_PROJ_BC = 12288
N_PROJ_G = 12288
LANE_A = 0
LANE_B = 16
LANE_LR = 32
N_SMALL = 128

LANES = 128
SUBLANES = 8
MXU_DIM = 256
VMEM_LIMIT = 56 * 1024 * 1024

TM = 1024
TN_PROJ = 1024
TN_UP = MXU_DIM
TM_UP = 1536
TM_DOWN = 2304
TN_DOWN = 1024
TK_DOWN = 768
TN_OUT = 512
TM_NORM = 256
MIXER_CFG_PROMPT = {"gdn": dict(bb=1, hb=8), "ret": dict(bb=2, hb=8), "gla": dict(bb=4, hb=4)}
MIXER_CFG_SAMPLE = {"gdn": dict(bb=4, hb=4), "ret": dict(bb=4, hb=4), "gla": dict(bb=2, hb=4)}


def _cparams(sem):
    return pltpu.CompilerParams(dimension_semantics=sem, vmem_limit_bytes=VMEM_LIMIT)


def _bf(x):
    return x.astype(BF16)


def _bdot(a, b):
    return jnp.dot(_bf(a), _bf(b), preferred_element_type=F32)


def _bdot_nt(a, b):
    return lax.dot_general(_bf(a), _bf(b), (((1,), (1,)), ((), ())), preferred_element_type=F32)


def _bdot_tn(a, b):
    return lax.dot_general(_bf(a), _bf(b), (((0,), (0,)), ((), ())), preferred_element_type=F32)


def _split3(x):
    hi = _bf(x)
    r1 = x - hi.astype(F32)
    mid = _bf(r1)
    lo = _bf(r1 - mid.astype(F32))
    return hi, mid, lo


def _dot01(m01, x):
    m = _bf(m01)
    hi, mid, lo = _split3(x)
    acc = jnp.dot(m, lo, preferred_element_type=F32)
    acc = acc + jnp.dot(m, mid, preferred_element_type=F32)
    return acc + jnp.dot(m, hi, preferred_element_type=F32)


def _dot01_tn(x, m01):
    m = _bf(m01)
    dn = (((0,), (0,)), ((), ()))
    hi, mid, lo = _split3(x)
    acc = lax.dot_general(lo, m, dn, preferred_element_type=F32)
    acc = acc + lax.dot_general(mid, m, dn, preferred_element_type=F32)
    return acc + lax.dot_general(hi, m, dn, preferred_element_type=F32)


def _silu(x):
    return x * jax.nn.sigmoid(x)


def _iota2(shape):
    return (lax.broadcasted_iota(jnp.int32, shape, 0), lax.broadcasted_iota(jnp.int32, shape, 1))


def _rms_kernel(x_ref, w_ref, o_ref):
    x = x_ref[...]
    y = x * lax.rsqrt(jnp.mean(x * x, axis=-1, keepdims=True) + EPS)
    o_ref[...] = (y * w_ref[...]).astype(o_ref.dtype)


def rmsnorm(x, w, out_dtype, row0=0, nrows=None):
    d = x.shape[1]
    nrows = x.shape[0] - row0 if nrows is None else nrows
    blk0 = row0 // TM_NORM
    return pl.pallas_call(
        _rms_kernel,
        grid=(nrows // TM_NORM,),
        in_specs=[pl.BlockSpec((TM_NORM, d), lambda i: (blk0 + i, 0)),
                  pl.BlockSpec((1, d), lambda i: (0, 0))],
        out_specs=pl.BlockSpec((TM_NORM, d), lambda i: (i, 0)),
        out_shape=jax.ShapeDtypeStruct((nrows, d), out_dtype),
        compiler_params=_cparams(("parallel",)),
        name="rmsnorm",
    )(x, w.reshape(1, d))


def _mm_kernel(x_ref, w_ref, o_ref):
    o_ref[...] = jnp.dot(x_ref[...], w_ref[...], preferred_element_type=F32).astype(o_ref.dtype)


def matmul(x, w, out_dtype, tn, name):
    m, k = x.shape
    n = w.shape[1]
    return pl.pallas_call(
        _mm_kernel,
        grid=(m // TM, n // tn),
        in_specs=[pl.BlockSpec((TM, k), lambda i, j: (i, 0)),
                  pl.BlockSpec((k, tn), lambda i, j: (0, j))],
        out_specs=pl.BlockSpec((TM, tn), lambda i, j: (i, j)),
        out_shape=jax.ShapeDtypeStruct((m, n), out_dtype),
        compiler_params=_cparams(("parallel", "parallel")),
        name=name,
    )(x, w)


def _swiglu_kernel(x_ref, wg_ref, wu_ref, o_ref, wg_bf, wu_bf):
    @pl.when(pl.program_id(1) == 0)
    def _():
        wg_bf[...] = _bf(wg_ref[...])
        wu_bf[...] = _bf(wu_ref[...])

    x = x_ref[...]
    g = jnp.dot(x, wg_bf[...], preferred_element_type=F32)
    u = jnp.dot(x, wu_bf[...], preferred_element_type=F32)
    o_ref[...] = (_silu(g) * u).astype(o_ref.dtype)


def swiglu_up(xn, wg, wu, layer):
    m, k = xn.shape
    n = wg.shape[2]
    return pl.pallas_call(
        _swiglu_kernel,
        grid=(n // TN_UP, m // TM_UP),
        in_specs=[pl.BlockSpec((TM_UP, k), lambda j, i: (i, 0)),
                  pl.BlockSpec((None, k, TN_UP), lambda j, i: (layer, 0, j)),
                  pl.BlockSpec((None, k, TN_UP), lambda j, i: (layer, 0, j))],
        out_specs=pl.BlockSpec((TM_UP, TN_UP), lambda j, i: (i, j)),
        out_shape=jax.ShapeDtypeStruct((m, n), BF16),
        scratch_shapes=[pltpu.VMEM((k, TN_UP), BF16)] * 2,
        compiler_params=_cparams(("arbitrary", "arbitrary")),
        name="swiglu_up",
    )(xn, wg, wu)


def _mm_res_kernel(h_ref, w_ref, x_ref, o_ref, *, scale, nk, k_last):
    kk = pl.program_id(2)
    tk = h_ref.shape[1]

    def part(kv):
        return scale * jnp.dot(h_ref[:, :kv], _bf(w_ref[:kv, :]), preferred_element_type=F32)

    @pl.when(kk == 0)
    def _():
        o_ref[...] = x_ref[...] + part(tk if nk > 1 else k_last)

    if nk > 2:
        @pl.when(jnp.logical_and(kk > 0, kk < nk - 1))
        def _():
            o_ref[...] += part(tk)

    if nk > 1:
        @pl.when(kk == nk - 1)
        def _():
            o_ref[...] += part(k_last)


def matmul_residual(h, w, layer, x, scale, tm, tn, tk, name):
    m, k = h.shape
    n = w.shape[2]
    nk = pl.cdiv(k, tk)
    k_last = k - (nk - 1) * tk
    return pl.pallas_call(
        functools.partial(_mm_res_kernel, scale=scale, nk=nk, k_last=k_last),
        grid=(m // tm, n // tn, nk),
        in_specs=[pl.BlockSpec((tm, tk), lambda i, j, kk: (i, kk)),
                  pl.BlockSpec((None, tk, tn), lambda i, j, kk: (layer, kk, j)),
                  pl.BlockSpec((tm, tn), lambda i, j, kk: (i, j))],
        out_specs=pl.BlockSpec((tm, tn), lambda i, j, kk: (i, j)),
        out_shape=jax.ShapeDtypeStruct((m, n), F32),
        compiler_params=_cparams(("parallel", "parallel", "arbitrary")),
        name=name,
    )(h, w, x)


def _merge_kernel(o_ref, w_ref, g_ref, out_ref, acc_ref):
    r = pl.program_id(2)
    y = jnp.dot(o_ref[...], _bf(w_ref[...]), preferred_element_type=F32)
    gated = jax.nn.sigmoid(g_ref[...].astype(F32)) * y

    @pl.when(r == 0)
    def _():
        acc_ref[...] = gated

    @pl.when(r > 0)
    def _():
        acc_ref[...] += gated

    @pl.when(r == N_BRANCH - 1)
    def _():
        out_ref[...] = acc_ref[...].astype(out_ref.dtype)


def merge_branches(o_all, w_branch, layer, proj):
    _, m, k = o_all.shape
    n = w_branch.shape[3]
    gate_blk0 = 0
    per_branch = D_MODEL // TN_PROJ
    return pl.pallas_call(
        _merge_kernel,
        grid=(m // TM, n // TN_PROJ, N_BRANCH),
        in_specs=[pl.BlockSpec((None, TM, k), lambda i, j, r: (r, i, 0)),
                  pl.BlockSpec((None, None, k, TN_PROJ), lambda i, j, r: (layer, r, 0, j)),
                  pl.BlockSpec((TM, TN_PROJ), lambda i, j, r: (i, gate_blk0 + r * per_branch + j))],
        out_specs=pl.BlockSpec((TM, TN_PROJ), lambda i, j, r: (i, j)),
        out_shape=jax.ShapeDtypeStruct((m, n), BF16),
        scratch_shapes=[pltpu.VMEM((TM, TN_PROJ), F32)],
        compiler_params=_cparams(("parallel", "parallel", "arbitrary")),
        name="merge_branches",
    )(o_all, w_branch, proj)


def _mixer_call(body, name, grid, in_specs, operands, scratch, *, branch, rows, wv, rblk0,
                o_buf, st_buf, out_layer, m_total, nb, heads, dk, dv, ns, hb):
    steps = grid[2]
    out_specs = [pl.BlockSpec((None, rows, wv), lambda b, g, n: (branch, rblk0 + b * steps + n, g)),
                 pl.BlockSpec((None, ns, hb, dk, dv), lambda b, g, n: (out_layer, b, g, 0, 0))]
    out_shape = [jax.ShapeDtypeStruct((N_BRANCH, m_total, D_MIX), BF16),
                 jax.ShapeDtypeStruct((DEPTH, nb, heads, dk, dv), F32)]
    aliases = {}
    operands = list(operands)
    in_specs = list(in_specs)
    n_alias = 0
    for out_idx, buf in ((0, o_buf), (1, st_buf)):
        if buf is not None:
            aliases[len(operands)] = out_idx
            operands.append(buf)
            in_specs.append(pl.BlockSpec(memory_space=pl.ANY))
            n_alias += 1

    def wrapped(*refs):
        n_in = len(operands)
        body(*refs[:n_in - n_alias], *refs[n_in:])

    return pl.pallas_call(
        wrapped,
        grid=grid,
        in_specs=in_specs,
        out_specs=out_specs,
        out_shape=out_shape,
        scratch_shapes=scratch,
        input_output_aliases=aliases,
        compiler_params=_cparams(("parallel", "parallel", "arbitrary")),
        name=name,
    )(*operands)


def _mixer_grid(row0, nb, seq, c, bb, head_groups):
    nchunks = seq // c
    chained = nchunks > 1
    if chained:
        assert nchunks % bb == 0
        grid = (nb, head_groups, nchunks // bb)
    else:
        assert nb % bb == 0
        grid = (nb // bb, head_groups, 1)
    rows = bb * c
    assert row0 % rows == 0
    rblk0 = row0 // rows
    steps = grid[2]

    def row_map(col0, w):
        return lambda b, g, n: (rblk0 + b * steps + n, col0 // w + g)

    return grid, chained, (1 if chained else bb), rows, rblk0, row_map


def _gdn_kernel(q_ref, k_ref, v_ref, z_ref, sm_ref, cwq_ref, cwk_ref, cwv_ref,
                csq_ref, csk_ref, csv_ref, hp_ref, nw_ref, s0_ref,
                o_ref, sout_ref, s_scr, xq, xk, xv, gct_scr, *, c, bb, hb, nchunks):
    g = pl.program_id(1)
    n = pl.program_id(2)
    pad = SUBLANES
    w2 = 2 * c

    @pl.when(n == 0)
    def _():
        s_scr[...] = s0_ref[...]
        xq[:, 0:pad, :] = csq_ref[...]
        xk[:, 0:pad, :] = csk_ref[...]
        xv[:, 0:pad, :] = csv_ref[...]

    ii, jw = _iota2((c, w2))
    jm = jnp.where(jw >= c, jw - c, jw)
    left = jw < c
    incl_w = ii >= jm
    strict_w = ii > jm
    eye_right = (jw == ii + c).astype(F32)
    cum_w = (ii <= jm).astype(F32)
    ic, jc = _iota2((c, c))
    lmat = (ic >= jc).astype(F32)
    lane = lax.broadcasted_iota(jnp.int32, (c, LANES), 1)
    a_log = hp_ref[0:1, :]
    dt_bias = hp_ref[1:2, :]
    nw = nw_ref[...]

    def conv(x_scr, bi, cw_ref):
        y = x_scr[bi, pl.ds(pad - 3, c), :] * cw_ref[0:1, :]
        for i in range(1, CONV_K):
            y = y + x_scr[bi, pl.ds(pad - 3 + i, c), :] * cw_ref[i:i + 1, :]
        return _silu(y)

    units = [(bi, hh) for bi in range(bb) for hh in range(hb)]
    st = {}
    for bi in range(bb):
        rows = slice(bi * c, (bi + 1) * c)
        xq[bi, pad:pad + c, :] = q_ref[rows, :]
        xk[bi, pad:pad + c, :] = k_ref[rows, :]
        xv[bi, pad:pad + c, :] = v_ref[rows, :]
        qa = conv(xq, bi, cwq_ref)
        ka = conv(xk, bi, cwk_ref)
        va = conv(xv, bi, cwv_ref)
        xq[bi, 0:pad, :] = xq[bi, c:c + pad, :]
        xk[bi, 0:pad, :] = xk[bi, c:c + pad, :]
        xv[bi, 0:pad, :] = xv[bi, c:c + pad, :]

        sm = sm_ref[rows, :]
        g_all = -jnp.exp(a_log) * jax.nn.softplus(sm + dt_bias)
        beta_all = jax.nn.sigmoid(sm)
        gc_all = _dot01(lmat, g_all)
        gct_scr[bi] = _dot01_tn(g_all, cum_w)
        for hh in range(hb):
            h = g * hb + hh
            cols = slice(hh * GDN_DK, (hh + 1) * GDN_DK)
            q = qa[:, cols]
            k = ka[:, cols]
            q = q * lax.rsqrt(jnp.sum(q * q, axis=-1, keepdims=True) + EPS) * (GDN_DK ** -0.5)
            k = k * lax.rsqrt(jnp.sum(k * k, axis=-1, keepdims=True) + EPS)
            gc_col = jnp.sum(jnp.where(lane == h + LANE_A, gc_all, 0.0), axis=-1, keepdims=True)
            beta = jnp.sum(jnp.where(lane == h + LANE_B, beta_all, 0.0), axis=-1, keepdims=True)
            gc = jnp.broadcast_to(gc_col, (c, GDN_DK))
            st[bi, hh] = dict(q=q, k=k, v=va[:, cols], beta=beta, gc_col=gc_col, gc=gc, kb=k * beta)

    for bi, hh in units:
        u_ = st[bi, hh]
        gc_row = gct_scr[bi, pl.ds(g * hb + hh, 1), :]
        decay_w = jnp.where(incl_w, jnp.exp(u_["gc_col"] - gc_row), 0.0)
        r = _bdot_nt(jnp.concatenate([u_["kb"], u_["q"]], axis=0),
                     jnp.concatenate([u_["k"], u_["k"]], axis=0))
        top = r[0:c, :] * decay_w
        u_["qk"] = r[c:w2, 0:c] * decay_w[:, 0:c]
        u_["wmat"] = jnp.where(left, jnp.where(strict_w, -top, 0.0), eye_right)

    for _ in range(c.bit_length() - 1):
        for unit in units:
            wmat = st[unit]["wmat"]
            st[unit]["wmat"] = _bdot(wmat[:, 0:c], wmat) + jnp.where(left, 0.0, wmat)

    for unit in units:
        u_ = st[unit]
        eg = jnp.exp(u_["gc"])
        u_["qe"] = u_["q"] * eg
        u_["uw"] = _bdot(u_["wmat"][:, c:w2],
                         jnp.concatenate([u_["v"] * u_["beta"], u_["kb"] * eg], axis=1))

    for bi, hh in units:
        u_ = st[bi, hh]
        u_["s_old"] = s_scr[bi, hh]
        u_["both"] = _bdot(jnp.concatenate([u_["uw"][:, GDN_DV:], u_["qe"]], axis=0), u_["s_old"])

    for unit in units:
        u_ = st[unit]
        u_["v_new"] = u_["uw"][:, 0:GDN_DV] - u_["both"][0:c, :]
        u_["o"] = u_["both"][c:w2, :] + _bdot(u_["qk"], u_["v_new"])

    outs = [[None] * bb for _ in range(hb)]
    for bi, hh in units:
        u_ = st[bi, hh]
        gl = u_["gc"][c - 1:c, :]
        s_scr[bi, hh] = (u_["s_old"] * jnp.exp(gl)
                         + _bdot_tn(u_["k"] * jnp.exp(gl - u_["gc"]), u_["v_new"]))
        o = u_["o"]
        o = o * lax.rsqrt(jnp.mean(o * o, axis=-1, keepdims=True) + EPS) * nw
        outs[hh][bi] = o * _silu(z_ref[bi * c:(bi + 1) * c, hh * GDN_DK:(hh + 1) * GDN_DK])

    for hh in range(hb):
        col = outs[hh][0] if bb == 1 else jnp.concatenate(outs[hh], axis=0)
        o_ref[:, hh * GDN_DV:(hh + 1) * GDN_DV] = col.astype(o_ref.dtype)

    @pl.when(n == nchunks - 1)
    def _():
        sout_ref[...] = s_scr[...]


def gdn_mixer(proj, small, conv_w, conv_state, head_params, norm_w, s0, layer,
              *, row0, nb, seq, c, bb, hb, o_buf=None, st_buf=None, out_layer=0):
    nchunks = seq // c
    assert bb == 1 or nchunks == 1
    assert c & (c - 1) == 0
    rows = bb * c
    wblk = hb * GDN_DK
    rblk0 = row0 // rows

    def row_map(col0):
        return lambda b, g, n: (rblk0 + b * nchunks + n, col0 // wblk + g)

    def cw_map(col0):
        return lambda b, g, n: (0, col0 // wblk + g)

    def cs_map(col0):
        return lambda b, g, n: (layer, b, 0, col0 // wblk + g)

    qkv_cols = (COL_QKV_A, COL_QKV_A + GDN_QK, COL_QKV_A + 2 * GDN_QK)
    in_specs = (
        [pl.BlockSpec((rows, wblk), row_map(c0)) for c0 in qkv_cols]
        + [pl.BlockSpec((rows, wblk), row_map(COL_Z_A)),
           pl.BlockSpec((rows, N_SMALL), lambda b, g, n: (rblk0 + b * nchunks + n, 0))]
        + [pl.BlockSpec((CONV_K, wblk), cw_map(c0)) for c0 in qkv_cols]
        + [pl.BlockSpec((None, bb, SUBLANES, wblk), cs_map(c0)) for c0 in qkv_cols]
        + [pl.BlockSpec((SUBLANES, LANES), lambda b, g, n: (0, 0)),
           pl.BlockSpec((1, GDN_DV), lambda b, g, n: (0, 0)),
           pl.BlockSpec((None, bb, hb, GDN_DK, GDN_DV), lambda b, g, n: (layer, b, g, 0, 0))])
    operands = (proj, proj, proj, proj, small, conv_w, conv_w, conv_w,
                conv_state, conv_state, conv_state, head_params, norm_w, s0)
    scratch = ([pltpu.VMEM((bb, hb, GDN_DK, GDN_DV), F32)]
               + [pltpu.VMEM((bb, SUBLANES + c, wblk), F32)] * 3
               + [pltpu.VMEM((bb, LANES, 2 * c), F32)])
    return _mixer_call(
        functools.partial(_gdn_kernel, c=c, bb=bb, hb=hb, nchunks=nchunks), "gdn_mixer",
        (nb // bb, GDN_HEADS // hb, nchunks), in_specs, operands, scratch,
        branch=0, rows=rows, wv=wblk, rblk0=rblk0, o_buf=o_buf, st_buf=st_buf,
        out_layer=out_layer, m_total=proj.shape[0], nb=nb, heads=GDN_HEADS, dk=GDN_DK, dv=GDN_DV, ns=bb, hb=hb)


def _ret_kernel(q_ref, k_ref, v_ref, gt_ref, tab_ref, lg_ref, s0_ref,
                o_ref, sout_ref, s_scr, *, c, bb, hb, chained):
    n = pl.program_id(2)

    @pl.when(n == 0)
    def _():
        s_scr[...] = s0_ref[...]

    ii, jj = _iota2((c, c))
    incl = ii >= jj
    dpos = (ii - jj).astype(F32)
    trow = lax.broadcasted_iota(jnp.int32, (c, RET_DK), 0).astype(F32)

    def rope(x, bi):
        rows = slice(bi * c, (bi + 1) * c) if chained else slice(0, c)
        return x * tab_ref[0, rows, :] + pltpu.roll(x, RET_DK // 2, axis=1) * tab_ref[1, rows, :]

    units = [(bi, hh) for bi in range(bb) for hh in range(hb)]
    hd = {}
    for hh in range(hb):
        lg = lg_ref[hh, 0:1, :]
        lg_k = lg[:, 0:RET_DK]
        hd[hh] = dict(decay=jnp.where(incl, jnp.exp(dpos * lg[:, 0:c]), 0.0),
                      q_scale=jnp.exp((trow + 1.0) * lg_k),
                      k_scale=jnp.exp((c - 1.0 - trow) * lg_k),
                      chunk_decay=jnp.exp(c * lg))
    st = {}
    for bi, hh in units:
        rows = slice(bi * c, (bi + 1) * c)
        q = rope(q_ref[rows, hh * RET_DK:(hh + 1) * RET_DK], bi)
        k = rope(k_ref[rows, hh * RET_DK:(hh + 1) * RET_DK], bi) * (RET_DK ** -0.5)
        st[bi, hh] = dict(q=q, k=k, scores=_bdot_nt(q, k) * hd[hh]["decay"])
    for bi, hh in units:
        u_ = st[bi, hh]
        v = v_ref[bi * c:(bi + 1) * c, hh * RET_DV:(hh + 1) * RET_DV]
        u_["intra"] = _bdot(u_["scores"], v)
        u_["kv"] = _bdot_tn(u_["k"] * hd[hh]["k_scale"], v)

    outs = [[None] * bb for _ in range(hb)]
    for bi in range(bb):
        si = 0 if chained else bi
        for hh in range(hb):
            u_ = st[bi, hh]
            u_["s_old"] = s_scr[si, hh]
            u_["inter"] = _bdot(u_["q"] * hd[hh]["q_scale"], u_["s_old"])
        for hh in range(hb):
            u_ = st[bi, hh]
            s_scr[si, hh] = u_["s_old"] * hd[hh]["chunk_decay"] + u_["kv"]
            o = u_["inter"] + u_["intra"]
            mu = jnp.mean(o, axis=-1, keepdims=True)
            var = jnp.mean(jnp.square(o - mu), axis=-1, keepdims=True)
            on = (o - mu) * lax.rsqrt(var + 1e-5)
            outs[hh][bi] = on * _silu(gt_ref[bi * c:(bi + 1) * c, hh * RET_DV:(hh + 1) * RET_DV])

    for hh in range(hb):
        col = outs[hh][0] if bb == 1 else jnp.concatenate(outs[hh], axis=0)
        o_ref[:, hh * RET_DV:(hh + 1) * RET_DV] = col.astype(o_ref.dtype)

    @pl.when(n == pl.num_programs(2) - 1)
    def _():
        sout_ref[...] = s_scr[...]


def ret_mixer(proj, rope_tab, lg_tab, s0, layer, *, row0, nb, seq, c, bb, hb,
              o_buf=None, st_buf=None, out_layer=0):
    grid, chained, ns, rows, rblk0, row_map = _mixer_grid(row0, nb, seq, c, bb, RET_HEADS // hb)
    wk = hb * RET_DK
    wv = hb * RET_DV
    tab_rows = rows if chained else c
    in_specs = [pl.BlockSpec((rows, wk), row_map(COL_Q_B, wk)),
                pl.BlockSpec((rows, wk), row_map(COL_K_B, wk)),
                pl.BlockSpec((rows, wv), row_map(COL_V_B, wv)),
                pl.BlockSpec((rows, wv), row_map(COL_G_B, wv)),
                pl.BlockSpec((2, tab_rows, RET_DK), lambda b, g, n: (0, n, 0)),
                pl.BlockSpec((hb, SUBLANES, RET_DV), lambda b, g, n: (g, 0, 0)),
                pl.BlockSpec((None, ns, hb, RET_DK, RET_DV), lambda b, g, n: (layer, b, g, 0, 0))]
    operands = (proj, proj, proj, proj, rope_tab, lg_tab, s0)
    scratch = [pltpu.VMEM((ns, hb, RET_DK, RET_DV), F32)]
    return _mixer_call(
        functools.partial(_ret_kernel, c=c, bb=bb, hb=hb, chained=chained), "ret_mixer",
        grid, in_specs, operands, scratch,
        branch=1, rows=rows, wv=wv, rblk0=rblk0, o_buf=o_buf, st_buf=st_buf,
        out_layer=out_layer, m_total=proj.shape[0], nb=nb, heads=RET_HEADS, dk=RET_DK, dv=RET_DV, ns=ns, hb=hb)


def _gla_kernel(q_ref, k_ref, v_ref, r_ref, sm_ref, wa_ref, ba_ref, nw_ref, s0_ref,
                o_ref, sout_ref, s_scr, *, c, bb, hb, chained):
    n = pl.program_id(2)

    @pl.when(n == 0)
    def _():
        s_scr[...] = s0_ref[...]

    ii, jj = _iota2((c, c))
    incl = ii >= jj
    lmat = incl.astype(F32)
    ones = jnp.ones((c, LANES), F32)
    nw = nw_ref[...]

    units = [(bi, hh) for bi in range(bb) for hh in range(hb)]
    shared = {}
    for bi in range(bb):
        rows = slice(bi * c, (bi + 1) * c)
        pre = jnp.dot(_bf(sm_ref[rows, :]), wa_ref[...], preferred_element_type=F32) + ba_ref[...]
        log_a_all = jax.nn.log_sigmoid(pre) / GLA_TAU
        shared[bi] = dict(
            bcum=_dot01(lmat, log_a_all),
            a_col=jnp.exp(_dot01_tn(log_a_all, ones)))
    st = {}
    for bi, hh in units:
        rows = slice(bi * c, (bi + 1) * c)
        kc = slice(hh * GLA_DK, (hh + 1) * GLA_DK)
        bcum = shared[bi]["bcum"][:, kc]
        b_last = bcum[c - 1:c, :]
        k = k_ref[rows, kc]
        qd = q_ref[rows, kc] * (GLA_DK ** -0.5) * jnp.exp(bcum)
        kd = k * jnp.exp(-bcum)
        st[bi, hh] = dict(qd=qd, kl=k * jnp.exp(b_last - bcum),
                          attn=jnp.where(incl, _bdot_nt(qd, kd), 0.0))
    for bi, hh in units:
        v = v_ref[bi * c:(bi + 1) * c, hh * GLA_DV:(hh + 1) * GLA_DV]
        st[bi, hh]["intra"] = _bdot(st[bi, hh]["attn"], v)

    outs = [[None] * bb for _ in range(hb)]
    for bi in range(bb):
        si = 0 if chained else bi
        rows = slice(bi * c, (bi + 1) * c)
        for hh in range(hb):
            u_ = st[bi, hh]
            u_["kv"] = _bdot_tn(u_["kl"], v_ref[rows, hh * GLA_DV:(hh + 1) * GLA_DV])
            u_["s_old"] = s_scr[si, hh]
            u_["inter"] = _bdot(u_["qd"], u_["s_old"])
        for hh in range(hb):
            u_ = st[bi, hh]
            a_col = shared[bi]["a_col"][hh * GLA_DK:(hh + 1) * GLA_DK, :]
            a_full = jnp.concatenate([a_col] * (GLA_DV // LANES), axis=1)
            s_scr[si, hh] = u_["s_old"] * a_full + u_["kv"]
            o = u_["inter"] + u_["intra"]
            on = o * lax.rsqrt(jnp.mean(o * o, axis=-1, keepdims=True) + EPS) * nw
            outs[hh][bi] = on * _silu(r_ref[rows, hh * GLA_DV:(hh + 1) * GLA_DV])

    for hh in range(hb):
        col = outs[hh][0] if bb == 1 else jnp.concatenate(outs[hh], axis=0)
        o_ref[:, hh * GLA_DV:(hh + 1) * GLA_DV] = col.astype(o_ref.dtype)

    @pl.when(n == pl.num_programs(2) - 1)
    def _():
        sout_ref[...] = s_scr[...]


def gla_mixer(proj, small, wa_pad, b_a, norm_w, s0, layer, *, row0, nb, seq, c, bb, hb,
              o_buf=None, st_buf=None, out_layer=0):
    grid, chained, ns, rows, rblk0, row_map = _mixer_grid(row0, nb, seq, c, bb, GLA_HEADS // hb)
    wk = hb * GLA_DK
    wv = hb * GLA_DV
    in_specs = [pl.BlockSpec((rows, wk), row_map(COL_Q_C, wk)),
                pl.BlockSpec((rows, wk), row_map(COL_K_C, wk)),
                pl.BlockSpec((rows, wv), row_map(COL_V_C, wv)),
                pl.BlockSpec((rows, wv), row_map(COL_R_C, wv)),
                pl.BlockSpec((rows, N_SMALL), lambda b, g, n: (rblk0 + b * grid[2] + n, 0)),
                pl.BlockSpec((N_SMALL, wk), lambda b, g, n: (0, g)),
                pl.BlockSpec((1, wk), lambda b, g, n: (0, g)),
                pl.BlockSpec((1, GLA_DV), lambda b, g, n: (0, 0)),
                pl.BlockSpec((None, ns, hb, GLA_DK, GLA_DV), lambda b, g, n: (layer, b, g, 0, 0))]
    operands = (proj, proj, proj, proj, small, wa_pad, b_a, norm_w, s0)
    scratch = [pltpu.VMEM((ns, hb, GLA_DK, GLA_DV), F32)]
    return _mixer_call(
        functools.partial(_gla_kernel, c=c, bb=bb, hb=hb, chained=chained), "gla_mixer",
        grid, in_specs, operands, scratch,
        branch=2, rows=rows, wv=wv, rblk0=rblk0, o_buf=o_buf, st_buf=st_buf,
        out_layer=out_layer, m_total=proj.shape[0], nb=nb, heads=GLA_HEADS, dk=GLA_DK, dv=GLA_DV, ns=ns, hb=hb)


def _rope_table(pos):
    half = RET_DK // 2
    inv = ROPE_BASE ** (-jnp.arange(half, dtype=F32) / half)
    ang = pos.astype(F32)[:, None] * inv[None, :]
    cos, sin = jnp.cos(ang), jnp.sin(ang)
    return jnp.stack([jnp.concatenate([cos, cos], axis=-1),
                      jnp.concatenate([-sin, sin], axis=-1)], axis=0)


def _split_w_in(w):
    a0 = CONV_DIM + D_MIX
    b0 = a0 + 2 * GDN_HEADS
    lr0 = b0 + 2 * RET_QK + 2 * D_MIX + 2 * GLA_QK + 2 * D_MIX
    g0 = lr0 + GLA_RANK
    assert (a0, lr0 - b0, w.shape[1] - g0) == (N_PROJ_A, N_PROJ_BC, N_PROJ_G)
    small = jnp.concatenate(
        [_bf(w[:, a0:b0]), _bf(w[:, lr0:g0]),
         jnp.zeros((w.shape[0], N_SMALL - 2 * GDN_HEADS - GLA_RANK), BF16)], axis=1)
    return _bf(w[:, :a0]), _bf(w[:, b0:lr0]), _bf(w[:, g0:]), small


def _ffn(x, norm_w, wg, wu, wd, layer):
    xn = rmsnorm(x, norm_w, BF16)
    h = swiglu_up(xn, wg, wu, layer)
    return matmul_residual(h, wd, layer, x, 0.5, TM_DOWN, TN_DOWN, TK_DOWN, "ffn_down")


def _tail_rows(row0, nb, seq):
    n_tail = min(seq, CONV_K - 1)
    return [row0 + b * seq + seq - n_tail + t for b in range(nb) for t in range(n_tail)], n_tail


def kernel(x_prompt, x_sample, state_gdn, state_gdn_conv, state_ret, state_gla, norm_ffn1, ffn1_gate, ffn1_up, ffn1_down, norm_mix, w_in, gdn_conv_w, gdn_a_log, gdn_dt_bias, gdn_norm_w, gla_w_a2, gla_b_a, gla_norm_w, w_branch, w_out, norm_ffn2, ffn2_gate, ffn2_up, ffn2_down, norm_final):
    bp, lp, _ = x_prompt.shape
    bs, ls, _ = x_sample.shape
    mp = bp * lp
    ms = bs * ls
    x = jnp.concatenate([x_prompt.reshape(mp, D_MODEL), x_sample.reshape(ms, D_MODEL)], axis=0)

    rope_p = _rope_table(jnp.arange(lp, dtype=jnp.int32))
    rope_s = _rope_table(PAST_LEN + jnp.arange(ls, dtype=jnp.int32))
    log_gamma = jnp.log1p(-jnp.exp2(-5.0 - jnp.arange(RET_HEADS, dtype=F32)))
    lg_tab = jnp.broadcast_to(log_gamma[:, None, None], (RET_HEADS, SUBLANES, RET_DV))

    zero_gdn = jnp.zeros((1, bp, GDN_HEADS, GDN_DK, GDN_DV), F32)
    zero_ret = jnp.zeros((1, bp, RET_HEADS, RET_DK, RET_DV), F32)
    zero_gla = jnp.zeros((1, bp, GLA_HEADS, GLA_DK, GLA_DV), F32)
    zero_conv = jnp.zeros((1, bp, SUBLANES, CONV_DIM), F32)
    conv_s_pad = jnp.pad(state_gdn_conv, ((0, 0), (0, 0), (SUBLANES - (CONV_K - 1), 0), (0, 0)))

    cp = math.gcd(lp, CHUNK)
    cs = math.gcd(ls, CHUNK)
    gcp = math.gcd(lp, GLA_CHUNK)
    gcs = math.gcd(ls, GLA_CHUNK)
    pc, sc = MIXER_CFG_PROMPT, MIXER_CFG_SAMPLE

    gdn_p = jnp.zeros((DEPTH, bp, GDN_HEADS, GDN_DK, GDN_DV), F32)
    ret_p = jnp.zeros((DEPTH, bp, RET_HEADS, RET_DK, RET_DV), F32)
    gla_p = jnp.zeros((DEPTH, bp, GLA_HEADS, GLA_DK, GLA_DV), F32)
    gdn_s = jnp.zeros((DEPTH, bs, GDN_HEADS, GDN_DK, GDN_DV), F32)
    ret_s = jnp.zeros((DEPTH, bs, RET_HEADS, RET_DK, RET_DV), F32)
    gla_s = jnp.zeros((DEPTH, bs, GLA_HEADS, GLA_DK, GLA_DV), F32)
    conv_p, conv_s = [], []
    for i in range(DEPTH):
        x = _ffn(x, norm_ffn1[i], ffn1_gate, ffn1_up, ffn1_down, i)

        xn = rmsnorm(x, norm_mix[i], BF16)
        w_a, w_bc, w_g, w_small = _split_w_in(w_in[i])
        proj_a = matmul(xn, w_a, F32, TN_PROJ, "in_proj_a")
        proj_bc = matmul(xn, w_bc, F32, TN_PROJ, "in_proj_bc")
        proj_g = matmul(xn, w_g, BF16, TN_PROJ, "in_proj_g")
        small = matmul(xn, w_small, F32, N_SMALL, "in_proj_small")

        head_params = jnp.zeros((SUBLANES, LANES), F32)
        head_params = head_params.at[0, :GDN_HEADS].set(gdn_a_log[i])
        head_params = head_params.at[1, :GDN_HEADS].set(gdn_dt_bias[i])
        gnw = gdn_norm_w[i].reshape(1, GDN_DV)
        wa_pad = jnp.zeros((N_SMALL, GLA_QK), F32).at[LANE_LR:LANE_LR + GLA_RANK].set(gla_w_a2[i])
        wa_pad = _bf(wa_pad)
        b_a = gla_b_a[i].reshape(1, GLA_QK)
        lnw = gla_norm_w[i].reshape(1, GLA_DV)

        o_all, gdn_p = gdn_mixer(proj_a, small, gdn_conv_w[i], zero_conv, head_params, gnw, zero_gdn, 0,
                                 row0=0, nb=bp, seq=lp, c=cp, **pc["gdn"],
                                 o_buf=jnp.zeros((N_BRANCH, mp + ms, D_MIX), BF16), st_buf=gdn_p, out_layer=i)
        o_all, gdn_s = gdn_mixer(proj_a, small, gdn_conv_w[i], conv_s_pad, head_params, gnw, state_gdn, i,
                                 row0=mp, nb=bs, seq=ls, c=cs, **sc["gdn"],
                                 o_buf=o_all, st_buf=gdn_s, out_layer=i)
        o_all, ret_p = ret_mixer(proj_bc, rope_p, lg_tab, zero_ret, 0,
                                 row0=0, nb=bp, seq=lp, c=cp, **pc["ret"],
                                 o_buf=o_all, st_buf=ret_p, out_layer=i)
        o_all, ret_s = ret_mixer(proj_bc, rope_s, lg_tab, state_ret, i,
                                 row0=mp, nb=bs, seq=ls, c=cs, **sc["ret"],
                                 o_buf=o_all, st_buf=ret_s, out_layer=i)
        o_all, gla_p = gla_mixer(proj_bc, small, wa_pad, b_a, lnw, zero_gla, 0,
                                 row0=0, nb=bp, seq=lp, c=gcp, **pc["gla"],
                                 o_buf=o_all, st_buf=gla_p, out_layer=i)
        o_all, gla_s = gla_mixer(proj_bc, small, wa_pad, b_a, lnw, state_gla, i,
                                 row0=mp, nb=bs, seq=ls, c=gcs, **sc["gla"],
                                 o_buf=o_all, st_buf=gla_s, out_layer=i)

        rows_p, nt_p = _tail_rows(0, bp, lp)
        rows_s, nt_s = _tail_rows(mp, bs, ls)
        tail_p = jnp.take(proj_a, jnp.asarray(rows_p, jnp.int32), axis=0)[:, :CONV_DIM].reshape(bp, nt_p, CONV_DIM)
        tail_s = jnp.take(proj_a, jnp.asarray(rows_s, jnp.int32), axis=0)[:, :CONV_DIM].reshape(bs, nt_s, CONV_DIM)
        conv_p.append(jnp.concatenate([zero_conv[0], tail_p], axis=1)[:, -(CONV_K - 1):, :])
        conv_s.append(jnp.concatenate([state_gdn_conv[i], tail_s], axis=1)[:, -(CONV_K - 1):, :])

        merged = merge_branches(o_all, w_branch, i, proj_g)
        x = matmul_residual(merged, w_out, i, x, 1.0, TM, TN_OUT, D_MODEL, "out_proj")

        x = _ffn(x, norm_ffn2[i], ffn2_gate, ffn2_up, ffn2_down, i)

    y_prompt = rmsnorm(x, norm_final, F32, 0, mp).reshape(bp, lp, D_MODEL)
    y_sample = rmsnorm(x, norm_final, F32, mp, ms).reshape(bs, ls, D_MODEL)
    return (y_prompt, y_sample, gdn_p, jnp.stack(conv_p, axis=0), ret_p, gla_p,
            gdn_s, jnp.stack(conv_s, axis=0), ret_s, gla_s)
```
